```python
import functools
import jax, jax.numpy as jnp
from jax import lax
import numpy as np

D_MODEL = 1024
BATCH = 1
SEQ = 16384
DEPTH = 2
DEC_BATCH = 16
DEC_SEQ = 64
PAST_LEN = 1024

CHUNK = 64
Q_BLOCK = 128
ROPE_THETA = 500000.0
RMS_EPS = 1e-6
D_HEAD = 64
SB_HEADS = 4
MLA_HEADS = 4
MLA_Q_LORA = 256
MLA_KV_LORA = 128
MLA_NOPE = 64
MLA_ROPE = 32
MLA_V = 64
DSA_HEADS = 4
DSA_ROT = D_HEAD // 4
IDX_HEADS = 4
IDX_DIM = 32
IDX_ROT = IDX_DIM // 4
TOPK_MAX = 256
N_BRANCH = 3
BRANCH_W = 256
D_FF = 2816
IN_SPLITS = (SB_HEADS * D_HEAD, SB_HEADS * D_HEAD, SB_HEADS * D_HEAD,
             MLA_Q_LORA, MLA_KV_LORA, MLA_ROPE,
             IDX_HEADS * IDX_DIM, IDX_HEADS, IDX_DIM,
             DSA_HEADS * D_HEAD, DSA_HEADS * D_HEAD, DSA_HEADS * D_HEAD,
             N_BRANCH * D_MODEL)
IN_WIDTH = sum(IN_SPLITS)

kernel_name = "hybrid_streaming_encoder_step"


def rmsnorm(x, g):
    xf = x.astype(jnp.float32)
    y = xf * lax.rsqrt(jnp.mean(xf * xf, axis=-1, keepdims=True) + RMS_EPS)
    return (y * g.astype(jnp.float32)).astype(x.dtype)


def swiglu(x, w_gu, w_dn):
    g, u = jnp.split(x @ w_gu, 2, axis=-1)
    return (jax.nn.silu(g) * u) @ w_dn


def rope(x, pos, rot):
    half = rot // 2
    inv = jnp.float32(ROPE_THETA) ** (-2.0 * jnp.arange(half, dtype=jnp.float32) / rot)
    ang = pos.astype(jnp.float32)[:, None] * inv[None, :]
    cos = jnp.cos(ang)[:, None, :].astype(x.dtype)
    sin = jnp.sin(ang)[:, None, :].astype(x.dtype)
    x1, x2 = x[..., :half], x[..., half:rot]
    return jnp.concatenate([x1 * cos - x2 * sin, x1 * sin + x2 * cos, x[..., rot:]], axis=-1)


def chunk_visible(q_pos, k_pos):
    return (k_pos[None, :] // CHUNK) <= (q_pos[:, None] // CHUNK)


def gather_rows(rows, idx):
    return jax.vmap(lambda r, i: r[i])(rows, idx)


def sb_block(q, k, v, q_pos, k_pos):
    z = jnp.einsum('bqhd,bkhd->bhqk', q, k).astype(jnp.float32) * (D_HEAD ** -0.5)
    vis = k_pos[None, :] < q_pos[:, None]
    log_beta = jax.nn.log_sigmoid(z)
    log_rest = jnp.where(vis, jax.nn.log_sigmoid(-z), 0.0)
    after = lax.cumsum(log_rest, axis=3, reverse=True) - log_rest
    w = jnp.where(vis, jnp.exp(log_beta + after), 0.0)
    return jnp.einsum('bhqk,bkhd->bqhd', w.astype(v.dtype), v)


def mla_block(q_nope, q_rope, k_nope, k_rope, v, q_pos, k_pos):
    s = (jnp.einsum('bqhd,bkhd->bhqk', q_nope, k_nope)
         + jnp.einsum('bqhr,bkr->bhqk', q_rope, k_rope)).astype(jnp.float32)
    s = s * ((MLA_NOPE + MLA_ROPE) ** -0.5)
    s = jnp.where(chunk_visible(q_pos, k_pos)[None, None], s, -jnp.inf)
    p = jax.nn.softmax(s, axis=-1)
    return jnp.einsum('bhqk,bkhd->bqhd', p.astype(v.dtype), v)


def dsa_block(q, iq, iw, k, v, ik, q_pos, k_pos, topk):
    r = jax.nn.relu(jnp.einsum('bqhd,bkd->bqhk', iq, ik).astype(jnp.float32))
    score = jnp.einsum('bqh,bqhk->bqk', iw.astype(jnp.float32), r)
    score = jnp.where(chunk_visible(q_pos, k_pos)[None], score, -jnp.inf)
    top_val, idx = lax.top_k(score, topk)
    valid = jnp.isfinite(top_val)
    k_sel = gather_rows(k, idx)
    v_sel = gather_rows(v, idx)
    s = jnp.einsum('bqhd,bqkhd->bqhk', q, k_sel).astype(jnp.float32) * (D_HEAD ** -0.5)
    s = jnp.where(valid[:, :, None, :], s, -jnp.inf)
    p = jax.nn.softmax(s, axis=-1)
    return jnp.einsum('bqhk,bqkhd->bqhd', p.astype(v.dtype), v_sel)


def sweep(fn, q_args, k_args, q_pos, k_pos, min_keys=0):
    n_q, n_k = q_pos.shape[0], k_pos.shape[0]
    k_off = n_k - n_q
    outs = []
    for lo in range(0, n_q, Q_BLOCK):
        hi = min(lo + Q_BLOCK, n_q)
        nk = min(n_k, max(k_off + hi, min_keys))
        outs.append(fn(*[a[:, lo:hi] for a in q_args], *[a[:, :nk] for a in k_args],
                       q_pos[lo:hi], k_pos[:nk]))
    return outs[0] if len(outs) == 1 else jnp.concatenate(outs, axis=1)


def encoder_layer(x, pos, past, topk,
                  g_f1_pre, g_f1_post, w_f1_gu, w_f1_dn,
                  g_mix_pre, g_mix_post, w_in, b_gate, g_q, g_kv, w_uq, w_ukv,
                  w_branch, w_out, g_f2_pre, g_f2_post, w_f2_gu, w_f2_dn):
    B, T, _ = x.shape
    h = x + 0.5 * rmsnorm(swiglu(rmsnorm(x, g_f1_pre), w_f1_gu, w_f1_dn), g_f1_post)
    u = rmsnorm(h, g_mix_pre)
    offs = [int(o) for o in np.cumsum(IN_SPLITS)[:-1]]
    (sb_q, sb_k, sb_v, mla_dq, mla_dkv, mla_kr, idx_q, idx_w, idx_k,
     dsa_q, dsa_k, dsa_v, gate_pre) = jnp.split(u @ w_in, offs, axis=-1)
    heads = lambda a, n: a.reshape(B, T, n, -1)
    sb_q, sb_k, sb_v = heads(sb_q, SB_HEADS), heads(sb_k, SB_HEADS), heads(sb_v, SB_HEADS)
    q_mla = heads(rmsnorm(mla_dq, g_q) @ w_uq, MLA_HEADS)
    q_nope = q_mla[..., :MLA_NOPE]
    q_rope = rope(q_mla[..., MLA_NOPE:], pos, MLA_ROPE)
    c_kv = rmsnorm(mla_dkv, g_kv)
    k_rope = rope(mla_kr[:, :, None, :], pos, MLA_ROPE)[:, :, 0]
    idx_q = rope(heads(idx_q, IDX_HEADS), pos, IDX_ROT)
    idx_w = idx_w * (IDX_HEADS ** -0.5)
    idx_k = rope(idx_k[:, :, None, :], pos, IDX_ROT)[:, :, 0]
    dsa_q = rope(heads(dsa_q, DSA_HEADS), pos, DSA_ROT)
    dsa_k = rope(heads(dsa_k, DSA_HEADS), pos, DSA_ROT)
    dsa_v = heads(dsa_v, DSA_HEADS)
    gates = jax.nn.sigmoid(gate_pre + b_gate).reshape(B, T, N_BRANCH, D_MODEL)

    new_rows = (sb_k, sb_v, c_kv, k_rope, dsa_k, dsa_v, idx_k)
    if past is None:
        keys, k_pos = new_rows, pos
    else:
        keys = tuple(jnp.concatenate([c, n], axis=1) for c, n in zip(past, new_rows))
        k_pos = jnp.arange(keys[0].shape[1], dtype=jnp.int32)
    sb_k_all, sb_v_all, ckv_all, krope_all, dk_all, dv_all, ik_all = keys
    kv = (ckv_all @ w_ukv).reshape(B, -1, MLA_HEADS, MLA_NOPE + MLA_V)
    k_nope, v_mla = kv[..., :MLA_NOPE], kv[..., MLA_NOPE:]

    o_sb = sweep(sb_block, (sb_q,), (sb_k_all, sb_v_all), pos, k_pos)
    o_mla = sweep(mla_block, (q_nope, q_rope), (k_nope, krope_all, v_mla), pos, k_pos)
    o_dsa = sweep(functools.partial(dsa_block, topk=topk), (dsa_q, idx_q, idx_w),
                  (dk_all, dv_all, ik_all), pos, k_pos, min_keys=topk)

    branches = jnp.stack([o.reshape(B, T, BRANCH_W) for o in (o_sb, o_mla, o_dsa)], axis=2)
    merged = jnp.sum(jnp.einsum('btnc,ncd->btnd', branches, w_branch) * gates, axis=2)
    h = h + rmsnorm(merged @ w_out, g_mix_post)
    y = h + 0.5 * rmsnorm(swiglu(rmsnorm(h, g_f2_pre), w_f2_gu, w_f2_dn), g_f2_post)
    return y, new_rows


def setup_inputs(seed: int = 0) -> dict:
    key = jax.random.key(seed)
    ks = iter(jax.random.split(key, 32))
    nrm = lambda shape, scale: jax.random.normal(next(ks), shape, jnp.float32) * scale
    gain = lambda shape: 1.0 + 0.02 * jax.random.normal(next(ks), shape, jnp.float32)
    return {
        "x_prompt": nrm((BATCH, SEQ, D_MODEL), 1.0),
        "x_sample": nrm((DEC_BATCH, DEC_SEQ, D_MODEL), 1.0),
        "cache_sb_k": nrm((DEPTH, DEC_BATCH, PAST_LEN, SB_HEADS, D_HEAD), 1.0),
        "cache_sb_v": nrm((DEPTH, DEC_BATCH, PAST_LEN, SB_HEADS, D_HEAD), 1.0),
        "cache_mla_ckv": nrm((DEPTH, DEC_BATCH, PAST_LEN, MLA_KV_LORA), 1.0),
        "cache_mla_krope": nrm((DEPTH, DEC_BATCH, PAST_LEN, MLA_ROPE), 1.0),
        "cache_dsa_k": nrm((DEPTH, DEC_BATCH, PAST_LEN, DSA_HEADS, D_HEAD), 1.0),
        "cache_dsa_v": nrm((DEPTH, DEC_BATCH, PAST_LEN, DSA_HEADS, D_HEAD), 1.0),
        "cache_dsa_idx_k": nrm((DEPTH, DEC_BATCH, PAST_LEN, IDX_DIM), 1.0),
        "g_ffn1_pre": gain((DEPTH, D_MODEL)),
        "g_ffn1_post": gain((DEPTH, D_MODEL)),
        "w_ffn1_gate_up": nrm((DEPTH, D_MODEL, 2 * D_FF), D_MODEL ** -0.5),
        "w_ffn1_down": nrm((DEPTH, D_FF, D_MODEL), D_FF ** -0.5),
        "g_mix_pre": gain((DEPTH, D_MODEL)),
        "g_mix_post": gain((DEPTH, D_MODEL)),
        "w_in": nrm((DEPTH, D_MODEL, IN_WIDTH), D_MODEL ** -0.5),
        "b_gate": nrm((DEPTH, N_BRANCH * D_MODEL), 0.02),
        "g_mla_q": gain((DEPTH, MLA_Q_LORA)),
        "g_mla_kv": gain((DEPTH, MLA_KV_LORA)),
        "w_mla_uq": nrm((DEPTH, MLA_Q_LORA, MLA_HEADS * (MLA_NOPE + MLA_ROPE)), MLA_Q_LORA ** -0.5),
        "w_mla_ukv": nrm((DEPTH, MLA_KV_LORA, MLA_HEADS * (MLA_NOPE + MLA_V)), MLA_KV_LORA ** -0.5),
        "w_branch": nrm((DEPTH, N_BRANCH, BRANCH_W, D_MODEL), BRANCH_W ** -0.5),
        "w_out": nrm((DEPTH, D_MODEL, D_MODEL), D_MODEL ** -0.5),
        "g_ffn2_pre": gain((DEPTH, D_MODEL)),
        "g_ffn2_post": gain((DEPTH, D_MODEL)),
        "w_ffn2_gate_up": nrm((DEPTH, D_MODEL, 2 * D_FF), D_MODEL ** -0.5),
        "w_ffn2_down": nrm((DEPTH, D_FF, D_MODEL), D_FF ** -0.5),
    }


def reference(x_prompt, x_sample, cache_sb_k, cache_sb_v, cache_mla_ckv, cache_mla_krope,
              cache_dsa_k, cache_dsa_v, cache_dsa_idx_k,
              g_ffn1_pre, g_ffn1_post, w_ffn1_gate_up, w_ffn1_down,
              g_mix_pre, g_mix_post, w_in, b_gate, g_mla_q, g_mla_kv, w_mla_uq, w_mla_ukv,
              w_branch, w_out, g_ffn2_pre, g_ffn2_post, w_ffn2_gate_up, w_ffn2_down):
    weights = (g_ffn1_pre, g_ffn1_post, w_ffn1_gate_up, w_ffn1_down,
               g_mix_pre, g_mix_post, w_in, b_gate, g_mla_q, g_mla_kv, w_mla_uq, w_mla_ukv,
               w_branch, w_out, g_ffn2_pre, g_ffn2_post, w_ffn2_gate_up, w_ffn2_down)
    caches = (cache_sb_k, cache_sb_v, cache_mla_ckv, cache_mla_krope,
              cache_dsa_k, cache_dsa_v, cache_dsa_idx_k)
    seq_p = x_prompt.shape[1]
    t_new = x_sample.shape[1]
    past_len = cache_sb_k.shape[2]
    pos_p = jnp.arange(seq_p, dtype=jnp.int32)
    pos_s = past_len + jnp.arange(t_new, dtype=jnp.int32)
    topk_p = min(TOPK_MAX, seq_p // 4)
    topk_s = min(TOPK_MAX, (past_len + t_new) // 4)

    y_prompt, y_sample = x_prompt, x_sample
    rows_p, rows_s = [], []
    for l in range(DEPTH):
        lw = [w[l] for w in weights]
        y_prompt, rp = encoder_layer(y_prompt, pos_p, None, topk_p, *lw)
        y_sample, rs = encoder_layer(y_sample, pos_s, tuple(c[l] for c in caches), topk_s, *lw)
        rows_p.append(rp)
        rows_s.append(rs)

    p_sb_k, p_sb_v, p_mla_ckv, p_mla_krope, p_dsa_k, p_dsa_v, p_dsa_idx_k = [
        jnp.stack([r[i] for r in rows_p]) for i in range(7)]
    s_sb_k, s_sb_v, s_mla_ckv, s_mla_krope, s_dsa_k, s_dsa_v, s_dsa_idx_k = [
        jnp.stack([r[i] for r in rows_s]) for i in range(7)]
    return (y_prompt, y_sample,
            p_sb_k, p_sb_v, p_mla_ckv, p_mla_krope, p_dsa_k, p_dsa_v, p_dsa_idx_k,
            s_sb_k, s_sb_v, s_mla_ckv, s_mla_krope, s_dsa_k, s_dsa_v, s_dsa_idx_k)
```

```python
import functools

import jax
import jax.numpy as jnp
from jax import lax
from jax.experimental import pallas as pl
from jax.experimental.pallas import tpu as pltpu

D_MODEL = 1024
D_FF = 2816
CHUNK_SHIFT = 6
ROPE_THETA = 500000.0
RMS_EPS = 1e-6
D_HEAD = 64
N_HEADS = 4
BRANCH_W = 256
MLA_Q_LORA = 256
MLA_KV_LORA = 128
MLA_NOPE = 64
MLA_ROPE = 32
IDX_DIM = 32
DSA_ROT = 16
IDX_ROT = 8
TOPK_MAX = 256
N_BRANCH = 3
IN_SPLITS = (256, 256, 256, 256, 128, 32, 128, 4, 32, 256, 256, 256, 3072)

C_SBQ, C_SBK, C_SBV, C_DQ, C_DK, C_DV, C_MDQ = 0, 256, 512, 768, 1024, 1280, 1536
C_MDKV, C_IQ, C_IK, C_KR, C_IW, W_MAIN = 1792, 1920, 2048, 2176, 2304, 2432
F_SBK, F_SBV, F_DK, F_DV, F_CKV, F_KR, F_IK, F_IW, W_F32 = 0, 256, 512, 768, 1024, 1152, 1280, 1408, 1536
B_SBQ, B_SBK, B_SBV, B_DQ, B_DK, B_DV, B_KC, B_IQ, B_IK, B_QA, W_BF = (
    0, 256, 512, 768, 1024, 1280, 1536, 1792, 1920, 2048, 3072)

VMEM_LIMIT_V7X = 56 * 1024 * 1024
EXP_UNDERFLOW = -110.0
NEG_BIG = -1e30
INT_MIN = -2 ** 31
INT_MAX = 2 ** 31 - 1

f32 = jnp.float32
bf16 = jnp.bfloat16
i32 = jnp.int32


def _rms(x, g):
    return x * lax.rsqrt(jnp.mean(x * x, axis=-1, keepdims=True) + RMS_EPS) * g


def _dot(a, b):
    return jnp.dot(a, b, preferred_element_type=f32)


def _dot_nt(a, b):
    return lax.dot_general(a, b, (((1,), (1,)), ((), ())), preferred_element_type=f32)


def _lane_group_mask(width, group, h):
    lane = lax.broadcasted_iota(i32, (1, width), 1)
    return (lane // group) == h


def _rope(x, c, s, half, per):
    w = x.shape[-1]
    lane = lax.broadcasted_iota(i32, (1, w), 1)
    first = (lane & (per - 1)) < half
    partner = jnp.where(first, pltpu.roll(x, w - half, 1), pltpu.roll(x, half, 1))
    return x * c + partner * s


def _cparams(sem):
    return pltpu.CompilerParams(dimension_semantics=sem, vmem_limit_bytes=VMEM_LIMIT_V7X)


def _ffn_kernel(x_ref, gpre_ref, gpost_ref, wg_ref, wu_ref, wd_ref, o_ref, xn_ref, acc_ref):
    c = pl.program_id(1)

    @pl.when(c == 0)
    def _():
        xn_ref[...] = _rms(x_ref[...], gpre_ref[...]).astype(bf16)
        acc_ref[...] = jnp.zeros_like(acc_ref)

    xn = xn_ref[...]
    g = _dot(xn, wg_ref[...])
    u = _dot(xn, wu_ref[...])
    a = (g / (1.0 + jnp.exp(-g)) * u).astype(bf16)
    acc_ref[...] += _dot(a, wd_ref[...])

    @pl.when(c == pl.num_programs(1) - 1)
    def _():
        o_ref[...] = x_ref[...] + 0.5 * _rms(acc_ref[...], gpost_ref[...])


def _ffn(x, g_pre, g_post, wg, wu, wd, tm, ff_chunk):
    n = x.shape[0]
    nc = D_FF // ff_chunk
    return pl.pallas_call(
        _ffn_kernel,
        out_shape=jax.ShapeDtypeStruct((n, D_MODEL), f32),
        grid=(n // tm, nc),
        in_specs=[
            pl.BlockSpec((tm, D_MODEL), lambda i, c: (i, 0)),
            pl.BlockSpec((1, D_MODEL), lambda i, c: (0, 0)),
            pl.BlockSpec((1, D_MODEL), lambda i, c: (0, 0)),
            pl.BlockSpec((D_MODEL, ff_chunk), lambda i, c: (0, c)),
            pl.BlockSpec((D_MODEL, ff_chunk), lambda i, c: (0, c)),
            pl.BlockSpec((ff_chunk, D_MODEL), lambda i, c: (c, 0)),
        ],
        out_specs=pl.BlockSpec((tm, D_MODEL), lambda i, c: (i, 0)),
        scratch_shapes=[pltpu.VMEM((tm, D_MODEL), bf16), pltpu.VMEM((tm, D_MODEL), f32)],
        compiler_params=_cparams(("parallel", "arbitrary")),
        name="ffn_half_step",
    )(x, g_pre, g_post, wg, wu, wd)


def _inproj_kernel(h_ref, tab_ref, gmix_ref, wmain_ref, gq_ref, gkv_ref, wuq_ref, wukt_ref,
                   f_ref, b_ref):
    u = _rms(h_ref[...], gmix_ref[...]).astype(bf16)
    proj = _dot(u, wmain_ref[...])
    tab = tab_ref[...]
    c_dsa = jnp.concatenate([tab[:, 0:128]] * 2, axis=-1)
    s_dsa = jnp.concatenate([tab[:, 128:256]] * 2, axis=-1)
    c_mla, s_mla = tab[:, 256:384], tab[:, 384:512]
    c_idx, s_idx = tab[:, 512:640], tab[:, 640:768]

    sb_q = proj[:, C_SBQ:C_SBQ + 256]
    sb_k = proj[:, C_SBK:C_SBK + 256]
    sb_v = proj[:, C_SBV:C_SBV + 256]
    dsa_q = _rope(proj[:, C_DQ:C_DQ + 256], c_dsa, s_dsa, DSA_ROT // 2, D_HEAD)
    dsa_k = _rope(proj[:, C_DK:C_DK + 256], c_dsa, s_dsa, DSA_ROT // 2, D_HEAD)
    dsa_v = proj[:, C_DV:C_DV + 256]
    idx_q = _rope(proj[:, C_IQ:C_IQ + 128], c_idx, s_idx, IDX_ROT // 2, IDX_DIM)
    idx_k = _rope(proj[:, C_IK:C_IK + 128], c_idx, s_idx, IDX_ROT // 2, IDX_DIM)
    k_rope = _rope(proj[:, C_KR:C_KR + 128], c_mla, s_mla, MLA_ROPE // 2, MLA_ROPE)
    idx_w = proj[:, C_IW:C_IW + 128]
    c_kv = _rms(proj[:, C_MDKV:C_MDKV + 128], gkv_ref[...])

    q_n = _rms(proj[:, C_MDQ:C_MDQ + 256], gq_ref[...]).astype(bf16)
    q_m = _dot(q_n, wuq_ref[...])
    q_nope = q_m[:, :256]
    q_rope = _rope(q_m[:, 256:384], c_mla, s_mla, MLA_ROPE // 2, MLA_ROPE)
    wukt = wukt_ref[...]
    for h in range(N_HEADS):
        q_nh = jnp.where(_lane_group_mask(256, MLA_NOPE, h), q_nope, 0.0).astype(bf16)
        q_lat = _dot(q_nh, wukt)
        q_rh = jnp.where(_lane_group_mask(128, MLA_ROPE, h), q_rope, 0.0)
        b_ref[:, B_QA + 256 * h:B_QA + 256 * h + 128] = q_lat.astype(bf16)
        b_ref[:, B_QA + 256 * h + 128:B_QA + 256 * (h + 1)] = q_rh.astype(bf16)

    f_ref[:, F_SBK:F_SBK + 256] = sb_k
    f_ref[:, F_SBV:F_SBV + 256] = sb_v
    f_ref[:, F_DK:F_DK + 256] = dsa_k
    f_ref[:, F_DV:F_DV + 256] = dsa_v
    f_ref[:, F_CKV:F_CKV + 128] = c_kv
    f_ref[:, F_KR:F_KR + 128] = k_rope
    f_ref[:, F_IK:F_IK + 128] = idx_k
    f_ref[:, F_IW:F_IW + 128] = idx_w

    b_ref[:, B_SBQ:B_SBQ + 256] = sb_q.astype(bf16)
    b_ref[:, B_SBK:B_SBK + 256] = sb_k.astype(bf16)
    b_ref[:, B_SBV:B_SBV + 256] = sb_v.astype(bf16)
    b_ref[:, B_DQ:B_DQ + 256] = dsa_q.astype(bf16)
    b_ref[:, B_DK:B_DK + 256] = dsa_k.astype(bf16)
    b_ref[:, B_DV:B_DV + 256] = dsa_v.astype(bf16)
    b_ref[:, B_KC:B_KC + 128] = c_kv.astype(bf16)
    b_ref[:, B_KC + 128:B_KC + 256] = k_rope.astype(bf16)
    b_ref[:, B_IQ:B_IQ + 128] = idx_q.astype(bf16)
    b_ref[:, B_IK:B_IK + 128] = idx_k.astype(bf16)


def _inproj(h, tab, g_mix, w_main, g_q, g_kv, w_uq, w_ukt, tm):
    n = h.shape[0]
    const = lambda i: (0, 0)
    return pl.pallas_call(
        _inproj_kernel,
        out_shape=(jax.ShapeDtypeStruct((n, W_F32), f32), jax.ShapeDtypeStruct((n, W_BF), bf16)),
        grid=(n // tm,),
        in_specs=[
            pl.BlockSpec((tm, D_MODEL), lambda i: (i, 0)),
            pl.BlockSpec((tm, 768), lambda i: (i, 0)),
            pl.BlockSpec((1, D_MODEL), const),
            pl.BlockSpec((D_MODEL, W_MAIN), const),
            pl.BlockSpec((1, MLA_Q_LORA), const),
            pl.BlockSpec((1, MLA_KV_LORA), const),
            pl.BlockSpec((MLA_Q_LORA, 384), const),
            pl.BlockSpec((256, MLA_KV_LORA), const),
        ],
        out_specs=(pl.BlockSpec((tm, W_F32), lambda i: (i, 0)),
                   pl.BlockSpec((tm, W_BF), lambda i: (i, 0))),
        compiler_params=_cparams(("parallel",)),
        name="mixer_in_projection",
    )(h, tab, g_mix, w_main, g_q, g_kv, w_uq, w_ukt)


def _merge_kernel(h_ref, osb_ref, omla_ref, odsa_ref, gpre_ref, gpost_ref, wgate_ref, bgate_ref,
                  wbr_ref, wout_ref, o_ref):
    h = h_ref[...]
    u = _rms(h, gpre_ref[...]).astype(bf16)
    merged = jnp.zeros_like(h)
    for b, o_b in enumerate((osb_ref, omla_ref, odsa_ref)):
        pre = _dot(u, wgate_ref[:, b * D_MODEL:(b + 1) * D_MODEL]) + bgate_ref[:, b * D_MODEL:(b + 1) * D_MODEL]
        gate = 1.0 / (1.0 + jnp.exp(-pre))
        merged = merged + _dot(o_b[...].astype(bf16), wbr_ref[b]) * gate
    o_ref[...] = h + _rms(_dot(merged.astype(bf16), wout_ref[...]), gpost_ref[...])


def _merge(h, o_sb, o_mla, o_dsa, g_pre, g_post, w_gate, b_gate, w_branch, w_out, tm):
    n = h.shape[0]
    const = lambda i: (0, 0)
    row = lambda w: pl.BlockSpec((tm, w), lambda i: (i, 0))
    return pl.pallas_call(
        _merge_kernel,
        out_shape=jax.ShapeDtypeStruct((n, D_MODEL), f32),
        grid=(n // tm,),
        in_specs=[
            row(D_MODEL), row(BRANCH_W), row(BRANCH_W), row(BRANCH_W),
            pl.BlockSpec((1, D_MODEL), const),
            pl.BlockSpec((1, D_MODEL), const),
            pl.BlockSpec((D_MODEL, N_BRANCH * D_MODEL), const),
            pl.BlockSpec((1, N_BRANCH * D_MODEL), const),
            pl.BlockSpec((N_BRANCH, BRANCH_W, D_MODEL), lambda i: (0, 0, 0)),
            pl.BlockSpec((D_MODEL, D_MODEL), const),
        ],
        out_specs=row(D_MODEL),
        compiler_params=_cparams(("parallel",)),
        name="gated_branch_merge",
    )(h, o_sb, o_mla, o_dsa, g_pre, g_post, w_gate, b_gate, w_branch, w_out)


def _visible_key_tiles(q0, tq, tk, n_kt):
    k_end = (((q0 + tq - 1) >> CHUNK_SHIFT) + 1) << CHUNK_SHIFT
    return jnp.minimum((k_end + tk - 1) // tk, n_kt)


def _stack_heads_masked(q, group):
    qf = q.astype(f32)
    w = q.shape[-1]
    return jnp.concatenate(
        [jnp.where(_lane_group_mask(w, group, h), qf, 0.0).astype(bf16) for h in range(N_HEADS)], axis=0)


def _take_head_lanes(stacked, tq):
    out = jnp.zeros((tq, stacked.shape[-1]), f32)
    for h in range(N_HEADS):
        out = jnp.where(_lane_group_mask(stacked.shape[-1], D_HEAD, h), stacked[h * tq:(h + 1) * tq], out)
    return out


def _sb_kernel(q_ref, k_ref, v_ref, o_ref, acc_ref, car_ref, *, tq, tk, k_off):
    q0 = k_off + pl.program_id(1) * tq
    qs = _stack_heads_masked(q_ref[...], D_HEAD)
    row = lax.broadcasted_iota(i32, (N_HEADS * tq, 1), 0)
    q_pos = q0 + (row & (tq - 1))
    col = lax.broadcasted_iota(i32, (1, tk), 1)
    later = (lax.broadcasted_iota(i32, (tk, tk), 0) > lax.broadcasted_iota(i32, (tk, tk), 1)).astype(bf16)
    acc_ref[...] = jnp.zeros_like(acc_ref)
    car_ref[...] = jnp.zeros_like(car_ref)

    def body(carry):
        kt, _ = carry
        ks = pl.multiple_of(kt * tk, tk)
        k = k_ref[pl.ds(ks, tk), :]
        v = v_ref[pl.ds(ks, tk), :]
        z = _dot_nt(qs, k) * (D_HEAD ** -0.5)
        vis = (ks + col) < q_pos
        sp = jnp.maximum(z, 0.0) + jnp.log1p(jnp.exp(-jnp.abs(z)))
        log_rest = jnp.where(vis, -sp, 0.0)
        hi = log_rest.astype(bf16)
        r1 = log_rest - hi.astype(f32)
        mid = r1.astype(bf16)
        lo = (r1 - mid.astype(f32)).astype(bf16)
        after = _dot(hi, later) + _dot(mid, later) + _dot(lo, later) + car_ref[...]
        w = jnp.where(vis, jnp.exp(z - sp + after), 0.0)
        acc_ref[...] += _dot(w.astype(bf16), v)
        car = car_ref[...] + jnp.sum(log_rest, axis=-1, keepdims=True)
        car_ref[...] = car
        done = (jnp.max(car) < EXP_UNDERFLOW).astype(i32)
        return kt - 1, done

    kt_last = (q0 + tq - 2) // tk
    lax.while_loop(lambda c: jnp.logical_and(c[0] >= 0, c[1] == 0), body, (kt_last, jnp.int32(0)))
    o_ref[...] = _take_head_lanes(acc_ref[...], tq)


def _mla_kernel(q_ref, kc_ref, wuv_ref, o_ref, m_ref, l_ref, acc_ref, *, tq, tk, k_off, n_kt):
    q0 = k_off + pl.program_id(1) * tq
    q = q_ref[...]
    qs = jnp.concatenate([q[:, 256 * h:256 * (h + 1)] for h in range(N_HEADS)], axis=0)
    row = lax.broadcasted_iota(i32, (N_HEADS * tq, 1), 0)
    q_chunk = (q0 + (row & (tq - 1))) >> CHUNK_SHIFT
    col = lax.broadcasted_iota(i32, (1, tk), 1)
    m_ref[...] = jnp.full_like(m_ref, NEG_BIG)
    l_ref[...] = jnp.zeros_like(l_ref)
    acc_ref[...] = jnp.zeros_like(acc_ref)

    def body(kt, _):
        ks = pl.multiple_of(kt * tk, tk)
        kc = kc_ref[pl.ds(ks, tk), :]
        s = _dot_nt(qs, kc) * ((MLA_NOPE + MLA_ROPE) ** -0.5)
        vis = ((ks + col) >> CHUNK_SHIFT) <= q_chunk
        s = jnp.where(vis, s, NEG_BIG)
        m_old = m_ref[...]
        m_new = jnp.maximum(m_old, jnp.max(s, axis=-1, keepdims=True))
        alpha = jnp.exp(m_old - m_new)
        p = jnp.where(vis, jnp.exp(s - m_new), 0.0)
        l_ref[...] = alpha * l_ref[...] + jnp.sum(p, axis=-1, keepdims=True)
        acc_ref[...] = alpha * acc_ref[...] + _dot(p.astype(bf16), kc[:, :MLA_KV_LORA])
        m_ref[...] = m_new
        return 0

    lax.fori_loop(0, _visible_key_tiles(q0, tq, tk, n_kt), body, 0)
    o_lat = (acc_ref[...] / l_ref[...]).astype(bf16)
    out = jnp.zeros((tq, BRANCH_W), f32)
    for h in range(N_HEADS):
        out = out + _dot(o_lat[h * tq:(h + 1) * tq], wuv_ref[h])
    o_ref[...] = out


def _dsa_kernel(q_ref, iq_ref, iw_ref, k_ref, v_ref, ik_ref, o_ref, key_ref, m_ref, l_ref, acc_ref,
                *, tq, tk, k_off, n_kt, topk):
    q0 = k_off + pl.program_id(1) * tq
    nkt = _visible_key_tiles(q0, tq, tk, n_kt)
    row = lax.broadcasted_iota(i32, (tq, 1), 0)
    q_chunk = (q0 + row) >> CHUNK_SHIFT
    col = lax.broadcasted_iota(i32, (1, tk), 1)
    lanes = tk // 128

    iqs = _stack_heads_masked(iq_ref[...], IDX_DIM)
    iw = iw_ref[...]
    w_h = [iw[:, h:h + 1] for h in range(N_HEADS)]

    def score_body(kt, _):
        ks = pl.multiple_of(kt * tk, tk)
        r = _dot_nt(iqs, ik_ref[pl.ds(ks, tk), :])
        score = jnp.zeros((tq, tk), f32)
        for h in range(N_HEADS):
            score = score + w_h[h] * jnp.maximum(r[h * tq:(h + 1) * tq], 0.0)
        score = jnp.where(score == 0.0, 0.0, score)
        bits = pltpu.bitcast(score, i32)
        key = jnp.where(bits < 0, bits ^ INT_MAX, bits)
        vis = ((ks + col) >> CHUNK_SHIFT) <= q_chunk
        key_ref[kt] = jnp.where(vis, key, INT_MIN)
        return 0

    lax.fori_loop(0, nkt, score_body, 0)

    def count(pred):
        def body(kt, acc):
            keys = key_ref[kt]
            for c in range(lanes):
                acc = acc + jnp.where(pred(keys[:, c * 128:(c + 1) * 128], kt * tk + c * 128), 1.0, 0.0)
            return acc
        acc = lax.fori_loop(0, nkt, body, jnp.zeros((tq, 128), f32))
        return jnp.sum(acc, axis=-1, keepdims=True)

    def count_ge(cand):
        cand_b = jnp.broadcast_to(cand, (tq, 128))
        return count(lambda keys, j0: keys >= cand_b)

    kf = float(topk)
    thr = jnp.where(count_ge(jnp.zeros((tq, 1), i32)) >= kf, 0, INT_MIN).astype(i32)

    def bit_body(b, thr):
        cand = thr + lax.shift_left(jnp.int32(1), 30 - b)
        return jnp.where(count_ge(cand) >= kf, cand, thr)

    thr = lax.fori_loop(0, 31, bit_body, thr)
    thr_b = jnp.broadcast_to(thr, (tq, 128))
    n_gt = count(lambda keys, j0: keys > thr_b)
    n_ge = count(lambda keys, j0: keys >= thr_b)
    need = kf - n_gt
    real = thr != INT_MIN
    lane = lax.broadcasted_iota(i32, (1, 128), 1)

    def tie_index():
        def body(b, x):
            cand = x + lax.shift_left(jnp.int32(1), 14 - b)
            cand_b = jnp.broadcast_to(cand, (tq, 128))
            n_eq = count(lambda keys, j0: jnp.logical_and(keys == thr_b, (j0 + lane) < cand_b))
            return jnp.where(n_eq < need, cand, x)
        return lax.fori_loop(0, 15, body, jnp.zeros((tq, 1), i32))

    has_ties = jnp.max(jnp.where(jnp.logical_and(real, n_ge > kf), 1.0, 0.0)) > 0.0
    last_eq = lax.cond(has_ties, tie_index, lambda: jnp.full((tq, 1), INT_MAX, i32))
    last_eq = jnp.where(real, last_eq, -1)

    q = q_ref[...].astype(f32)
    q_h = [jnp.where(_lane_group_mask(256, D_HEAD, h), q, 0.0).astype(bf16) for h in range(N_HEADS)]
    m_ref[...] = jnp.full_like(m_ref, NEG_BIG)
    l_ref[...] = jnp.zeros_like(l_ref)
    acc_ref[...] = jnp.zeros_like(acc_ref)

    def att_body(kt, _):
        ks = pl.multiple_of(kt * tk, tk)
        k = k_ref[pl.ds(ks, tk), :]
        v = v_ref[pl.ds(ks, tk), :]
        keys = key_ref[kt]
        sel = jnp.logical_or(keys > thr, jnp.logical_and(keys == thr, (ks + col) <= last_eq))
        for h in range(N_HEADS):
            s = jnp.where(sel, _dot_nt(q_h[h], k) * (D_HEAD ** -0.5), NEG_BIG)
            m_old = m_ref[h]
            m_new = jnp.maximum(m_old, jnp.max(s, axis=-1, keepdims=True))
            alpha = jnp.exp(m_old - m_new)
            p = jnp.where(sel, jnp.exp(s - m_new), 0.0)
            l_ref[h] = alpha * l_ref[h] + jnp.sum(p, axis=-1, keepdims=True)
            acc_ref[h] = alpha * acc_ref[h] + _dot(p.astype(bf16), v)
            m_ref[h] = m_new
        return 0

    lax.fori_loop(0, nkt, att_body, 0)
    out = jnp.zeros((tq, BRANCH_W), f32)
    for h in range(N_HEADS):
        out = jnp.where(_lane_group_mask(BRANCH_W, D_HEAD, h), acc_ref[h] / l_ref[h], out)
    o_ref[...] = out


def _resident(shape, index_map):
    return pl.BlockSpec(shape, index_map, pipeline_mode=pl.Buffered(1))


def _mixers(geom, bf_all, f_all, keys, w_uv):
    batch, tq, nq, tk, lk = geom["batch"], geom["tq"], geom["nq"], geom["tk"], geom["lk"]
    k_off, topk, q_blk0 = geom["k_off"], geom["topk"], geom["q_row0"] // geom["tq"]
    n_kt = lk // tk
    bf3 = bf_all.reshape(1, *bf_all.shape)
    f3 = f_all.reshape(1, *f_all.shape)

    def q_spec(width, col):
        return pl.BlockSpec((None, tq, width), lambda b, i: (0, q_blk0 + b * nq + i, col // width))

    def key_operand(name, packed_col, width):
        if keys is None:
            return bf3, _resident((None, lk, width), lambda b, i: (0, 0, packed_col // width))
        return keys[name], _resident((None, lk, width), lambda b, i: (b, 0, 0))

    out_shape = jax.ShapeDtypeStruct((batch, nq * tq, BRANCH_W), f32)
    out_spec = pl.BlockSpec((None, tq, BRANCH_W), lambda b, i: (b, i, 0))
    params = _cparams(("parallel", "arbitrary"))

    sbk, sbk_spec = key_operand("sb_k", B_SBK, 256)
    sbv, sbv_spec = key_operand("sb_v", B_SBV, 256)
    o_sb = pl.pallas_call(
        functools.partial(_sb_kernel, tq=tq, tk=128, k_off=k_off),
        out_shape=out_shape, grid=(batch, nq),
        in_specs=[q_spec(256, B_SBQ), sbk_spec, sbv_spec], out_specs=out_spec,
        scratch_shapes=[pltpu.VMEM((N_HEADS * tq, 256), f32), pltpu.VMEM((N_HEADS * tq, 1), f32)],
        compiler_params=params, name="sb_attention",
    )(bf3, sbk, sbv)

    kc, kc_spec = key_operand("kc", B_KC, 256)
    o_mla = pl.pallas_call(
        functools.partial(_mla_kernel, tq=tq, tk=tk, k_off=k_off, n_kt=n_kt),
        out_shape=out_shape, grid=(batch, nq),
        in_specs=[q_spec(1024, B_QA), kc_spec,
                  pl.BlockSpec((N_HEADS, MLA_KV_LORA, BRANCH_W), lambda b, i: (0, 0, 0))],
        out_specs=out_spec,
        scratch_shapes=[pltpu.VMEM((N_HEADS * tq, 1), f32), pltpu.VMEM((N_HEADS * tq, 1), f32),
                        pltpu.VMEM((N_HEADS * tq, MLA_KV_LORA), f32)],
        compiler_params=params, name="mla_attention",
    )(bf3, kc, w_uv)

    dk, dk_spec = key_operand("dsa_k", B_DK, 256)
    dv, dv_spec = key_operand("dsa_v", B_DV, 256)
    ik, ik_spec = key_operand("ik", B_IK, 128)
    o_dsa = pl.pallas_call(
        functools.partial(_dsa_kernel, tq=tq, tk=tk, k_off=k_off, n_kt=n_kt, topk=topk),
        out_shape=out_shape, grid=(batch, nq),
        in_specs=[q_spec(256, B_DQ), q_spec(128, B_IQ),
                  pl.BlockSpec((None, tq, 128), lambda b, i: (0, q_blk0 + b * nq + i, F_IW // 128)),
                  dk_spec, dv_spec, ik_spec],
        out_specs=out_spec,
        scratch_shapes=[pltpu.VMEM((n_kt, tq, tk), i32),
                        pltpu.VMEM((N_HEADS, tq, 1), f32), pltpu.VMEM((N_HEADS, tq, 1), f32),
                        pltpu.VMEM((N_HEADS, tq, BRANCH_W), f32)],
        compiler_params=params, name="dsa_attention",
    )(bf3, bf3, f3, dk, dv, ik)
    return o_sb, o_mla, o_dsa


def _rope_tables(pos):
    n = pos.shape[0]

    def pair(rot, per):
        half = rot // 2
        inv = jnp.float32(ROPE_THETA) ** (-2.0 * jnp.arange(half, dtype=f32) / rot)
        ang = pos.astype(f32)[:, None] * inv[None, :]
        cos, sin = jnp.cos(ang), jnp.sin(ang)
        c = jnp.concatenate([cos, cos, jnp.ones((n, per - rot), f32)], axis=-1)
        s = jnp.concatenate([-sin, sin, jnp.zeros((n, per - rot), f32)], axis=-1)
        return jnp.tile(c, (1, 128 // per)), jnp.tile(s, (1, 128 // per))

    return jnp.concatenate([*pair(DSA_ROT, D_HEAD), *pair(MLA_ROPE, MLA_ROPE), *pair(IDX_ROT, IDX_DIM)], axis=-1)


def _prep_layer_weights(w_in, w_uq, w_ukv):
    offs = [0]
    for s in IN_SPLITS:
        offs.append(offs[-1] + s)
    (sb_q, sb_k, sb_v, mla_dq, mla_dkv, mla_kr, idx_q, idx_w, idx_k,
     dsa_q, dsa_k, dsa_v, gate) = [w_in[:, offs[j]:offs[j + 1]] for j in range(len(IN_SPLITS))]
    small = jnp.concatenate([idx_w * 0.5, jnp.zeros((D_MODEL, 124), f32)], axis=-1)
    w_main = jnp.concatenate(
        [sb_q, sb_k, sb_v, dsa_q, dsa_k, dsa_v, mla_dq, mla_dkv, idx_q,
         jnp.tile(idx_k, (1, 4)), jnp.tile(mla_kr, (1, 4)), small], axis=-1).astype(bf16)
    uq = w_uq.reshape(MLA_Q_LORA, N_HEADS, MLA_NOPE + MLA_ROPE)
    w_uq_perm = jnp.concatenate([uq[:, :, :MLA_NOPE].reshape(MLA_Q_LORA, 256),
                                 uq[:, :, MLA_NOPE:].reshape(MLA_Q_LORA, 128)], axis=-1).astype(bf16)
    ukv = w_ukv.reshape(MLA_KV_LORA, N_HEADS, 128)
    w_ukt = jnp.transpose(ukv[:, :, :MLA_NOPE], (1, 2, 0)).reshape(256, MLA_KV_LORA).astype(bf16)
    w_uv = jnp.zeros((N_HEADS, MLA_KV_LORA, BRANCH_W), f32)
    for h in range(N_HEADS):
        w_uv = w_uv.at[h, :, D_HEAD * h:D_HEAD * (h + 1)].set(ukv[:, h, MLA_NOPE:])
    return w_main, gate.astype(bf16), w_uq_perm, w_ukt, w_uv.astype(bf16)


def _pick_tile(n, candidates):
    for c in candidates:
        if n % c == 0:
            return c
    raise ValueError(f"no tile in {candidates} divides {n}")


def kernel(x_prompt, x_sample, cache_sb_k, cache_sb_v, cache_mla_ckv, cache_mla_krope, cache_dsa_k, cache_dsa_v, cache_dsa_idx_k, g_ffn1_pre, g_ffn1_post, w_ffn1_gate_up, w_ffn1_down, g_mix_pre, g_mix_post, w_in, b_gate, g_mla_q, g_mla_kv, w_mla_uq, w_mla_ukv, w_branch, w_out, g_ffn2_pre, g_ffn2_post, w_ffn2_gate_up, w_ffn2_down):
    bp, tp, _ = x_prompt.shape
    bs, ts, _ = x_sample.shape
    depth = w_in.shape[0]
    past = cache_sb_k.shape[2]
    assert bp == 1 and tp % 128 == 0 and ts % 64 == 0 and ts <= 128
    n_p, n_s = bp * tp, bs * ts
    n = n_p + n_s
    tm = _pick_tile(n, (512, 256, 128, 64))
    ff_chunk = D_FF // 2

    pos = jnp.concatenate([jnp.arange(tp, dtype=i32), jnp.tile(past + jnp.arange(ts, dtype=i32), bs)])
    tab = _rope_tables(pos)

    tk_p = _pick_tile(tp, (512, 256, 128))
    lk_s = -(-(past + ts) // 128) * 128
    geom_p = dict(batch=1, tq=128, nq=tp // 128, tk=tk_p, lk=tp, k_off=0,
                  topk=min(TOPK_MAX, tp // 4), q_row0=0)
    geom_s = dict(batch=bs, tq=ts, nq=1, tk=128, lk=lk_s, k_off=past,
                  topk=min(TOPK_MAX, (past + ts) // 4), q_row0=n_p)

    y = jnp.concatenate([x_prompt.reshape(n_p, D_MODEL), x_sample.reshape(n_s, D_MODEL)], axis=0)
    row = lambda g: g.reshape(1, -1)
    rows_p, rows_s = [], []
    for l in range(depth):
        w_main, w_gate, w_uq_perm, w_ukt, w_uv = _prep_layer_weights(w_in[l], w_mla_uq[l], w_mla_ukv[l])
        h = _ffn(y, row(g_ffn1_pre[l]), row(g_ffn1_post[l]),
                 w_ffn1_gate_up[l][:, :D_FF].astype(bf16), w_ffn1_gate_up[l][:, D_FF:].astype(bf16),
                 w_ffn1_down[l].astype(bf16), tm, ff_chunk)
        f_all, bf_all = _inproj(h, tab, row(g_mix_pre[l]), w_main, row(g_mla_q[l]), row(g_mla_kv[l]),
                                w_uq_perm, w_ukt, tm)

        def with_cache(cache_rows, new_cols, width):
            new = bf_all[n_p:, new_cols:new_cols + width].reshape(bs, ts, width)
            allk = jnp.concatenate([cache_rows.astype(bf16), new], axis=1)
            return jnp.pad(allk, ((0, 0), (0, lk_s - past - ts), (0, 0)))

        keys_s = {
            "sb_k": with_cache(cache_sb_k[l].reshape(bs, past, 256), B_SBK, 256),
            "sb_v": with_cache(cache_sb_v[l].reshape(bs, past, 256), B_SBV, 256),
            "kc": with_cache(jnp.concatenate([cache_mla_ckv[l], jnp.tile(cache_mla_krope[l], (1, 1, 4))], axis=-1),
                             B_KC, 256),
            "dsa_k": with_cache(cache_dsa_k[l].reshape(bs, past, 256), B_DK, 256),
            "dsa_v": with_cache(cache_dsa_v[l].reshape(bs, past, 256), B_DV, 256),
            "ik": with_cache(jnp.tile(cache_dsa_idx_k[l], (1, 1, 4)), B_IK, 128),
        }
        o_p = _mixers(geom_p, bf_all, f_all, None, w_uv)
        o_s = _mixers(geom_s, bf_all, f_all, keys_s, w_uv)
        o_sb, o_mla, o_dsa = [jnp.concatenate([a.reshape(n_p, BRANCH_W), b.reshape(n_s, BRANCH_W)], axis=0)
                              for a, b in zip(o_p, o_s)]
        h2 = _merge(h, o_sb, o_mla, o_dsa, row(g_mix_pre[l]), row(g_mix_post[l]), w_gate, row(b_gate[l]),
                    w_branch[l].astype(bf16), w_out[l].astype(bf16), tm)
        y = _ffn(h2, row(g_ffn2_pre[l]), row(g_ffn2_post[l]),
                 w_ffn2_gate_up[l][:, :D_FF].astype(bf16), w_ffn2_gate_up[l][:, D_FF:].astype(bf16),
                 w_ffn2_down[l].astype(bf16), tm, ff_chunk)

        def new_rows(lo, hi, cnt, shape):
            return f_all[lo:hi, :].reshape(cnt + (W_F32,)), shape

        for rows, lo, hi, lead in ((rows_p, 0, n_p, (bp, tp)), (rows_s, n_p, n, (bs, ts))):
            blk = f_all[lo:hi]
            rows.append((
                blk[:, F_SBK:F_SBK + 256].reshape(*lead, N_HEADS, D_HEAD),
                blk[:, F_SBV:F_SBV + 256].reshape(*lead, N_HEADS, D_HEAD),
                blk[:, F_CKV:F_CKV + MLA_KV_LORA].reshape(*lead, MLA_KV_LORA),
                blk[:, F_KR:F_KR + MLA_ROPE].reshape(*lead, MLA_ROPE),
                blk[:, F_DK:F_DK + 256].reshape(*lead, N_HEADS, D_HEAD),
                blk[:, F_DV:F_DV + 256].reshape(*lead, N_HEADS, D_HEAD),
                blk[:, F_IK:F_IK + IDX_DIM].reshape(*lead, IDX_DIM),
            ))

    stack = lambda rows: [jnp.stack([r[j] for r in rows]) for j in range(7)]
    return (y[:n_p].reshape(bp, tp, D_MODEL), y[n_p:].reshape(bs, ts, D_MODEL), *stack(rows_p), *stack(rows_s))
```

```python
import functools

import jax
import jax.numpy as jnp
from jax import lax
from jax.experimental import pallas as pl
from jax.experimental.pallas import tpu as pltpu

D_MODEL = 1024
D_FF = 2816
CHUNK_SHIFT = 6
ROPE_THETA = 500000.0
RMS_EPS = 1e-6
D_HEAD = 64
N_HEADS = 4
BRANCH_W = 256
MLA_Q_LORA = 256
MLA_KV_LORA = 128
MLA_NOPE = 64
MLA_ROPE = 32
IDX_DIM = 32
DSA_ROT = 16
IDX_ROT = 8
TOPK_MAX = 256
N_BRANCH = 3
IN_SPLITS = (256, 256, 256, 256, 128, 32, 128, 4, 32, 256, 256, 256, 3072)

C_SBQ, C_SBK, C_SBV, C_DQ, C_DK, C_DV, C_MDQ = 0, 256, 512, 768, 1024, 1280, 1536
C_MDKV, C_IQ, C_IK, C_KR, C_IW, W_MAIN = 1792, 1920, 2048, 2176, 2304, 2432
F_SBK, F_SBV, F_DK, F_DV, F_CKV, F_KR, F_IK, F_IW, W_F32 = 0, 256, 512, 768, 1024, 1152, 1280, 1408, 1536
B_SBQ, B_SBK, B_SBV, B_DQ, B_DK, B_DV, B_KC, B_IQ, B_IK, B_QA, W_BF = (
    0, 256, 512, 768, 1024, 1280, 1536, 1792, 1920, 2048, 3072)

VMEM_LIMIT_V7X = 56 * 1024 * 1024
EXP_UNDERFLOW = -110.0
NEG_BIG = -1e30
INT_MIN = -2 ** 31
INT_MAX = 2 ** 31 - 1
LOG2E = 1.4426950408889634
MLA_Q_SCALE = (MLA_NOPE + MLA_ROPE) ** -0.5 * LOG2E
DSA_Q_SCALE = D_HEAD ** -0.5 * LOG2E
SB_Q_SCALE = D_HEAD ** -0.5
QLANE_TQ = 128
COUNT_ROWS = 128
QLANE_COL_BLOCKS = 1
MLA_ROW_BLOCKS = 2

f32 = jnp.float32
bf16 = jnp.bfloat16
i32 = jnp.int32
i16 = jnp.int16


def _rms(x, g):
    return x * lax.rsqrt(jnp.mean(x * x, axis=-1, keepdims=True) + RMS_EPS) * g


def _dot(a, b):
    return jnp.dot(a, b, preferred_element_type=f32)


def _dot_nt(a, b):
    return lax.dot_general(a, b, (((1,), (1,)), ((), ())), preferred_element_type=f32)


def _lane_group_mask(width, group, h):
    lane = lax.broadcasted_iota(i32, (1, width), 1)
    return (lane // group) == h


def _rope(x, c, s, half, per):
    w = x.shape[-1]
    lane = lax.broadcasted_iota(i32, (1, w), 1)
    first = (lane & (per - 1)) < half
    partner = jnp.where(first, pltpu.roll(x, w - half, 1), pltpu.roll(x, half, 1))
    return x * c + partner * s


def _cparams(sem):
    return pltpu.CompilerParams(dimension_semantics=sem, vmem_limit_bytes=VMEM_LIMIT_V7X)


def _ffn_kernel(x_ref, gpre_ref, gpost_ref, wg_ref, wu_ref, wd_ref, o_ref, xn_ref, acc_ref):
    c = pl.program_id(1)

    @pl.when(c == 0)
    def _():
        xn_ref[...] = _rms(x_ref[...], gpre_ref[...]).astype(bf16)
        acc_ref[...] = jnp.zeros_like(acc_ref)

    xn = xn_ref[...]
    g = _dot(xn, wg_ref[...])
    u = _dot(xn, wu_ref[...])
    a = (g / (1.0 + jnp.exp(-g)) * u).astype(bf16)
    acc_ref[...] += _dot(a, wd_ref[...])

    @pl.when(c == pl.num_programs(1) - 1)
    def _():
        o_ref[...] = x_ref[...] + 0.5 * _rms(acc_ref[...], gpost_ref[...])


def _ffn(x, g_pre, g_post, wg, wu, wd, tm, ff_chunk):
    n = x.shape[0]
    nc = D_FF // ff_chunk
    return pl.pallas_call(
        _ffn_kernel,
        out_shape=jax.ShapeDtypeStruct((n, D_MODEL), f32),
        grid=(n // tm, nc),
        in_specs=[
            pl.BlockSpec((tm, D_MODEL), lambda i, c: (i, 0)),
            pl.BlockSpec((1, D_MODEL), lambda i, c: (0, 0)),
            pl.BlockSpec((1, D_MODEL), lambda i, c: (0, 0)),
            pl.BlockSpec((D_MODEL, ff_chunk), lambda i, c: (0, c)),
            pl.BlockSpec((D_MODEL, ff_chunk), lambda i, c: (0, c)),
            pl.BlockSpec((ff_chunk, D_MODEL), lambda i, c: (c, 0)),
        ],
        out_specs=pl.BlockSpec((tm, D_MODEL), lambda i, c: (i, 0)),
        scratch_shapes=[pltpu.VMEM((tm, D_MODEL), bf16), pltpu.VMEM((tm, D_MODEL), f32)],
        compiler_params=_cparams(("parallel", "arbitrary")),
        name="ffn_half_step",
    )(x, g_pre, g_post, wg, wu, wd)


def _inproj_kernel(h_ref, tab_ref, gmix_ref, wmain_ref, gq_ref, gkv_ref, wuq_ref, wukt_ref,
                   f_ref, b_ref):
    u = _rms(h_ref[...], gmix_ref[...]).astype(bf16)
    proj = _dot(u, wmain_ref[...])
    tab = tab_ref[...]
    c_dsa = jnp.concatenate([tab[:, 0:128]] * 2, axis=-1)
    s_dsa = jnp.concatenate([tab[:, 128:256]] * 2, axis=-1)
    c_mla, s_mla = tab[:, 256:384], tab[:, 384:512]
    c_idx, s_idx = tab[:, 512:640], tab[:, 640:768]

    sb_q = proj[:, C_SBQ:C_SBQ + 256]
    sb_k = proj[:, C_SBK:C_SBK + 256]
    sb_v = proj[:, C_SBV:C_SBV + 256]
    dsa_q = _rope(proj[:, C_DQ:C_DQ + 256], c_dsa, s_dsa, DSA_ROT // 2, D_HEAD)
    dsa_k = _rope(proj[:, C_DK:C_DK + 256], c_dsa, s_dsa, DSA_ROT // 2, D_HEAD)
    dsa_v = proj[:, C_DV:C_DV + 256]
    idx_q = _rope(proj[:, C_IQ:C_IQ + 128], c_idx, s_idx, IDX_ROT // 2, IDX_DIM)
    idx_k = _rope(proj[:, C_IK:C_IK + 128], c_idx, s_idx, IDX_ROT // 2, IDX_DIM)
    k_rope = _rope(proj[:, C_KR:C_KR + 128], c_mla, s_mla, MLA_ROPE // 2, MLA_ROPE)
    idx_w = proj[:, C_IW:C_IW + 128]
    c_kv = _rms(proj[:, C_MDKV:C_MDKV + 128], gkv_ref[...])

    q_n = _rms(proj[:, C_MDQ:C_MDQ + 256], gq_ref[...]).astype(bf16)
    q_m = _dot(q_n, wuq_ref[...])
    q_nope = q_m[:, :256]
    q_rope = _rope(q_m[:, 256:384], c_mla, s_mla, MLA_ROPE // 2, MLA_ROPE)
    wukt = wukt_ref[...]
    for h in range(N_HEADS):
        q_nh = jnp.where(_lane_group_mask(256, MLA_NOPE, h), q_nope, 0.0).astype(bf16)
        q_lat = _dot(q_nh, wukt)
        q_rh = jnp.where(_lane_group_mask(128, MLA_ROPE, h), q_rope, 0.0)
        b_ref[:, B_QA + 256 * h:B_QA + 256 * h + 128] = (q_lat * MLA_Q_SCALE).astype(bf16)
        b_ref[:, B_QA + 256 * h + 128:B_QA + 256 * (h + 1)] = (q_rh * MLA_Q_SCALE).astype(bf16)

    f_ref[:, F_SBK:F_SBK + 256] = sb_k
    f_ref[:, F_SBV:F_SBV + 256] = sb_v
    f_ref[:, F_DK:F_DK + 256] = dsa_k
    f_ref[:, F_DV:F_DV + 256] = dsa_v
    f_ref[:, F_CKV:F_CKV + 128] = c_kv
    f_ref[:, F_KR:F_KR + 128] = k_rope
    f_ref[:, F_IK:F_IK + 128] = idx_k
    f_ref[:, F_IW:F_IW + 128] = idx_w

    b_ref[:, B_SBQ:B_SBQ + 256] = (sb_q * SB_Q_SCALE).astype(bf16)
    b_ref[:, B_SBK:B_SBK + 256] = sb_k.astype(bf16)
    b_ref[:, B_SBV:B_SBV + 256] = sb_v.astype(bf16)
    b_ref[:, B_DQ:B_DQ + 256] = (dsa_q * DSA_Q_SCALE).astype(bf16)
    b_ref[:, B_DK:B_DK + 256] = dsa_k.astype(bf16)
    b_ref[:, B_DV:B_DV + 256] = dsa_v.astype(bf16)
    b_ref[:, B_KC:B_KC + 128] = c_kv.astype(bf16)
    b_ref[:, B_KC + 128:B_KC + 256] = k_rope.astype(bf16)
    b_ref[:, B_IQ:B_IQ + 128] = idx_q.astype(bf16)
    b_ref[:, B_IK:B_IK + 128] = idx_k.astype(bf16)


def _inproj(h, tab, g_mix, w_main, g_q, g_kv, w_uq, w_ukt, tm):
    n = h.shape[0]
    const = lambda i: (0, 0)
    return pl.pallas_call(
        _inproj_kernel,
        out_shape=(jax.ShapeDtypeStruct((n, W_F32), f32), jax.ShapeDtypeStruct((n, W_BF), bf16)),
        grid=(n // tm,),
        in_specs=[
            pl.BlockSpec((tm, D_MODEL), lambda i: (i, 0)),
            pl.BlockSpec((tm, 768), lambda i: (i, 0)),
            pl.BlockSpec((1, D_MODEL), const),
            pl.BlockSpec((D_MODEL, W_MAIN), const),
            pl.BlockSpec((1, MLA_Q_LORA), const),
            pl.BlockSpec((1, MLA_KV_LORA), const),
            pl.BlockSpec((MLA_Q_LORA, 384), const),
            pl.BlockSpec((256, MLA_KV_LORA), const),
        ],
        out_specs=(pl.BlockSpec((tm, W_F32), lambda i: (i, 0)),
                   pl.BlockSpec((tm, W_BF), lambda i: (i, 0))),
        compiler_params=_cparams(("parallel",)),
        name="mixer_in_projection",
    )(h, tab, g_mix, w_main, g_q, g_kv, w_uq, w_ukt)


def _merge_kernel(h_ref, osb_ref, omla_ref, odsa_ref, gpre_ref, gpost_ref, wgate_ref, bgate_ref,
                  wbr_ref, wout_ref, o_ref):
    h = h_ref[...]
    u = _rms(h, gpre_ref[...]).astype(bf16)
    merged = jnp.zeros_like(h)
    for b, o_b in enumerate((osb_ref, omla_ref, odsa_ref)):
        pre = _dot(u, wgate_ref[:, b * D_MODEL:(b + 1) * D_MODEL]) + bgate_ref[:, b * D_MODEL:(b + 1) * D_MODEL]
        gate = 1.0 / (1.0 + jnp.exp(-pre))
        merged = merged + _dot(o_b[...].astype(bf16), wbr_ref[b]) * gate
    o_ref[...] = h + _rms(_dot(merged.astype(bf16), wout_ref[...]), gpost_ref[...])


def _merge(h, o_sb, o_mla, o_dsa, g_pre, g_post, w_gate, b_gate, w_branch, w_out, tm):
    n = h.shape[0]
    const = lambda i: (0, 0)
    row = lambda w: pl.BlockSpec((tm, w), lambda i: (i, 0))
    return pl.pallas_call(
        _merge_kernel,
        out_shape=jax.ShapeDtypeStruct((n, D_MODEL), f32),
        grid=(n // tm,),
        in_specs=[
            row(D_MODEL), row(BRANCH_W), row(BRANCH_W), row(BRANCH_W),
            pl.BlockSpec((1, D_MODEL), const),
            pl.BlockSpec((1, D_MODEL), const),
            pl.BlockSpec((D_MODEL, N_BRANCH * D_MODEL), const),
            pl.BlockSpec((1, N_BRANCH * D_MODEL), const),
            pl.BlockSpec((N_BRANCH, BRANCH_W, D_MODEL), lambda i: (0, 0, 0)),
            pl.BlockSpec((D_MODEL, D_MODEL), const),
        ],
        out_specs=row(D_MODEL),
        compiler_params=_cparams(("parallel",)),
        name="gated_branch_merge",
    )(h, o_sb, o_mla, o_dsa, g_pre, g_post, w_gate, b_gate, w_branch, w_out)


def _visible_key_tiles(q0, tq, tk, n_kt):
    k_end = (((q0 + tq - 1) >> CHUNK_SHIFT) + 1) << CHUNK_SHIFT
    return jnp.minimum((k_end + tk - 1) // tk, n_kt)


def _stack_heads_masked(q, group):
    qf = q.astype(f32)
    w = q.shape[-1]
    return jnp.concatenate(
        [jnp.where(_lane_group_mask(w, group, h), qf, 0.0).astype(bf16) for h in range(N_HEADS)], axis=0)


def _take_head_lanes(stacked, tq):
    out = jnp.zeros((tq, stacked.shape[-1]), f32)
    for h in range(N_HEADS):
        out = jnp.where(_lane_group_mask(stacked.shape[-1], D_HEAD, h), stacked[h * tq:(h + 1) * tq], out)
    return out


def _sb_kernel(q_ref, k_ref, v_ref, o_ref, acc_ref, car_ref, *, tq, tk, k_off):
    q0 = k_off + pl.program_id(1) * tq
    qs = _stack_heads_masked(q_ref[...], D_HEAD)
    row = lax.broadcasted_iota(i32, (N_HEADS * tq, 1), 0)
    q_pos = q0 + (row & (tq - 1))
    col = lax.broadcasted_iota(i32, (1, tk), 1)
    later = (lax.broadcasted_iota(i32, (tk, tk), 0) > lax.broadcasted_iota(i32, (tk, tk), 1)).astype(bf16)
    acc_ref[...] = jnp.zeros_like(acc_ref)
    car_ref[...] = jnp.zeros_like(car_ref)

    def body(carry):
        kt, _ = carry
        ks = pl.multiple_of(kt * tk, tk)
        k = k_ref[pl.ds(ks, tk), :]
        v = v_ref[pl.ds(ks, tk), :]
        z = _dot_nt(qs, k)
        vis = (ks + col) < q_pos
        sp = jnp.maximum(z, 0.0) + jnp.log1p(jnp.exp(-jnp.abs(z)))
        log_rest = jnp.where(vis, -sp, 0.0)
        hi = log_rest.astype(bf16)
        r1 = log_rest - hi.astype(f32)
        mid = r1.astype(bf16)
        lo = (r1 - mid.astype(f32)).astype(bf16)
        after = _dot(hi, later) + _dot(mid, later) + _dot(lo, later) + car_ref[...]
        w = jnp.where(vis, jnp.exp(z - sp + after), 0.0)
        acc_ref[...] += _dot(w.astype(bf16), v)
        car = car_ref[...] + jnp.sum(log_rest, axis=-1, keepdims=True)
        car_ref[...] = car
        done = (jnp.max(car) < EXP_UNDERFLOW).astype(i32)
        return kt - 1, done

    kt_last = (q0 + tq - 2) // tk
    lax.while_loop(lambda c: jnp.logical_and(c[0] >= 0, c[1] == 0), body, (kt_last, jnp.int32(0)))
    o_ref[...] = _take_head_lanes(acc_ref[...], tq)


def _mla_kernel(q_ref, kc_ref, wuv_ref, o_ref, m_ref, l_ref, acc_ref, *, tq, tk, k_off, n_kt):
    q0 = k_off + pl.program_id(1) * tq
    q = q_ref[...]
    qs = jnp.concatenate([q[:, 256 * h:256 * (h + 1)] for h in range(N_HEADS)], axis=0)
    row = lax.broadcasted_iota(i32, (N_HEADS * tq, 1), 0)
    q_chunk = (q0 + (row & (tq - 1))) >> CHUNK_SHIFT
    col = lax.broadcasted_iota(i32, (1, tk), 1)
    m_ref[...] = jnp.full_like(m_ref, NEG_BIG)
    l_ref[...] = jnp.zeros_like(l_ref)
    acc_ref[...] = jnp.zeros_like(acc_ref)
    rb = N_HEADS * tq // MLA_ROW_BLOCKS

    def step(kt, masked):
        ks = pl.multiple_of(kt * tk, tk)
        kc = kc_ref[pl.ds(ks, tk), :]
        for r in range(MLA_ROW_BLOCKS):
            rows = slice(r * rb, (r + 1) * rb)
            s = _dot_nt(qs[rows], kc)
            if masked:
                s = jnp.where(((ks + col) >> CHUNK_SHIFT) <= q_chunk[rows], s, NEG_BIG)
            m_old = m_ref[rows]
            m_new = jnp.maximum(m_old, jnp.max(s, axis=-1, keepdims=True))
            alpha = jnp.exp2(m_old - m_new)
            p = jnp.exp2(s - m_new)
            l_ref[rows] = alpha * l_ref[rows] + jnp.sum(p, axis=-1, keepdims=True)
            acc_ref[rows] = alpha * acc_ref[rows] + _dot(p.astype(bf16), kc[:, :MLA_KV_LORA])
            m_ref[rows] = m_new
        return 0

    n_full = (((q0 >> CHUNK_SHIFT) + 1) << CHUNK_SHIFT) // tk
    lax.fori_loop(0, n_full, lambda kt, c: step(kt, False), 0)
    lax.fori_loop(n_full, _visible_key_tiles(q0, tq, tk, n_kt), lambda kt, c: step(kt, True), 0)
    o_lat = (acc_ref[...] / l_ref[...]).astype(bf16)
    out = jnp.zeros((tq, BRANCH_W), f32)
    for h in range(N_HEADS):
        out = out + _dot(o_lat[h * tq:(h + 1) * tq], wuv_ref[h])
    o_ref[...] = out


def _mla_qlane_kernel(q_ref, kc_ref, kct_ref, wuv_ref, o_ref, m_ref, l_ref, acc_ref, *, tk, n_kt):
    tq = QLANE_TQ
    q0 = pl.program_id(1) * tq
    q = q_ref[...]
    qs = jnp.concatenate([q[:, 256 * h:256 * (h + 1)] for h in range(N_HEADS)], axis=0)
    lane = lax.broadcasted_iota(i32, (1, N_HEADS * tq), 1)
    q_chunk = (q0 + (lane & (tq - 1))) >> CHUNK_SHIFT
    k_row = lax.broadcasted_iota(i32, (tk, 1), 0)
    m_ref[...] = jnp.full_like(m_ref, NEG_BIG)
    l_ref[...] = jnp.zeros_like(l_ref)
    acc_ref[...] = jnp.zeros_like(acc_ref)

    cb = N_HEADS * tq // QLANE_COL_BLOCKS

    def step(kt, masked):
        ks = pl.multiple_of(kt * tk, tk)
        kc = kc_ref[pl.ds(ks, tk), :]
        kct = kct_ref[kt]
        for c in range(QLANE_COL_BLOCKS):
            cols = slice(c * cb, (c + 1) * cb)
            s = _dot_nt(kc, qs[cols])
            if masked:
                s = jnp.where(((ks + k_row) >> CHUNK_SHIFT) <= q_chunk[:, cols], s, NEG_BIG)
            m_old = m_ref[:, cols]
            m_new = jnp.maximum(m_old, jnp.max(s, axis=0, keepdims=True))
            alpha = jnp.exp2(m_old - m_new)
            p = jnp.exp2(s - m_new)
            l_ref[:, cols] = alpha * l_ref[:, cols] + jnp.sum(p, axis=0, keepdims=True)
            acc_ref[:, cols] = alpha * acc_ref[:, cols] + _dot(kct, p.astype(bf16))
            m_ref[:, cols] = m_new
        return 0

    n_full = (((q0 >> CHUNK_SHIFT) + 1) << CHUNK_SHIFT) // tk
    lax.fori_loop(0, n_full, lambda kt, c: step(kt, False), 0)
    lax.fori_loop(n_full, _visible_key_tiles(q0, tq, tk, n_kt), lambda kt, c: step(kt, True), 0)
    o_lat_t = acc_ref[...] / l_ref[...]
    out = jnp.zeros((tq, BRANCH_W), f32)
    for h in range(N_HEADS):
        out = out + _dot(o_lat_t[:, h * tq:(h + 1) * tq].T.astype(bf16), wuv_ref[h])
    o_ref[...] = out


def _dsa_kernel(q_ref, iq_ref, iw_ref, k_ref, v_ref, ik_ref, o_ref, key_ref, m_ref, l_ref, acc_ref,
                *, tq, tk, k_off, n_kt, topk):
    q0 = k_off + pl.program_id(1) * tq
    nkt = _visible_key_tiles(q0, tq, tk, n_kt)
    row = lax.broadcasted_iota(i32, (tq, 1), 0)
    q_chunk = (q0 + row) >> CHUNK_SHIFT
    col = lax.broadcasted_iota(i32, (1, tk), 1)
    lanes = tk // 128

    iqs = _stack_heads_masked(iq_ref[...], IDX_DIM)
    iw = iw_ref[...]
    w_h = [iw[:, h:h + 1] for h in range(N_HEADS)]

    def score_body(kt, _):
        ks = pl.multiple_of(kt * tk, tk)
        r = _dot_nt(iqs, ik_ref[pl.ds(ks, tk), :])
        score = jnp.zeros((tq, tk), f32)
        for h in range(N_HEADS):
            score = score + w_h[h] * jnp.maximum(r[h * tq:(h + 1) * tq], 0.0)
        score = jnp.where(score == 0.0, 0.0, score)
        bits = pltpu.bitcast(score, i32)
        key = jnp.where(bits < 0, bits ^ INT_MAX, bits)
        vis = ((ks + col) >> CHUNK_SHIFT) <= q_chunk
        key_ref[kt] = jnp.where(vis, key, INT_MIN)
        return 0

    lax.fori_loop(0, nkt, score_body, 0)

    def count(pred):
        def body(kt, acc):
            keys = key_ref[kt]
            for c in range(lanes):
                acc = acc + jnp.where(pred(keys[:, c * 128:(c + 1) * 128], kt * tk + c * 128), 1.0, 0.0)
            return acc
        acc = lax.fori_loop(0, nkt, body, jnp.zeros((tq, 128), f32))
        return jnp.sum(acc, axis=-1, keepdims=True)

    def count_ge(cand):
        cand_b = jnp.broadcast_to(cand, (tq, 128))
        return count(lambda keys, j0: keys >= cand_b)

    kf = float(topk)
    thr = jnp.where(count_ge(jnp.zeros((tq, 1), i32)) >= kf, 0, INT_MIN).astype(i32)

    def bit_body(b, thr):
        cand = thr + lax.shift_left(jnp.int32(1), 30 - b)
        return jnp.where(count_ge(cand) >= kf, cand, thr)

    thr = lax.fori_loop(0, 31, bit_body, thr)
    thr_b = jnp.broadcast_to(thr, (tq, 128))
    n_gt = count(lambda keys, j0: keys > thr_b)
    n_ge = count(lambda keys, j0: keys >= thr_b)
    need = kf - n_gt
    real = thr != INT_MIN
    lane = lax.broadcasted_iota(i32, (1, 128), 1)

    def tie_index():
        def body(b, x):
            cand = x + lax.shift_left(jnp.int32(1), 14 - b)
            cand_b = jnp.broadcast_to(cand, (tq, 128))
            n_eq = count(lambda keys, j0: jnp.logical_and(keys == thr_b, (j0 + lane) < cand_b))
            return jnp.where(n_eq < need, cand, x)
        return lax.fori_loop(0, 15, body, jnp.zeros((tq, 1), i32))

    has_ties = jnp.max(jnp.where(jnp.logical_and(real, n_ge > kf), 1.0, 0.0)) > 0.0
    last_eq = lax.cond(has_ties, tie_index, lambda: jnp.full((tq, 1), INT_MAX, i32))
    last_eq = jnp.where(real, last_eq, -1)

    q = q_ref[...].astype(f32)
    q_h = [jnp.where(_lane_group_mask(256, D_HEAD, h), q, 0.0).astype(bf16) for h in range(N_HEADS)]
    m_ref[...] = jnp.full_like(m_ref, NEG_BIG)
    l_ref[...] = jnp.zeros_like(l_ref)
    acc_ref[...] = jnp.zeros_like(acc_ref)

    def att_body(kt, _):
        ks = pl.multiple_of(kt * tk, tk)
        k = k_ref[pl.ds(ks, tk), :]
        v = v_ref[pl.ds(ks, tk), :]
        keys = key_ref[kt]
        sel = jnp.logical_or(keys > thr, jnp.logical_and(keys == thr, (ks + col) <= last_eq))
        for h in range(N_HEADS):
            s = jnp.where(sel, _dot_nt(q_h[h], k), NEG_BIG)
            m_old = m_ref[h]
            m_new = jnp.maximum(m_old, jnp.max(s, axis=-1, keepdims=True))
            alpha = jnp.exp2(m_old - m_new)
            p = jnp.where(sel, jnp.exp2(s - m_new), 0.0)
            l_ref[h] = alpha * l_ref[h] + jnp.sum(p, axis=-1, keepdims=True)
            acc_ref[h] = alpha * acc_ref[h] + _dot(p.astype(bf16), v)
            m_ref[h] = m_new
        return 0

    lax.fori_loop(0, nkt, att_body, 0)
    out = jnp.zeros((tq, BRANCH_W), f32)
    for h in range(N_HEADS):
        out = jnp.where(_lane_group_mask(BRANCH_W, D_HEAD, h), acc_ref[h] / l_ref[h], out)
    o_ref[...] = out


def _dsa_qlane_kernel(q_ref, iq_ref, iw_ref, k_ref, vt_ref, ik_ref, o_ref, key_ref, hi_ref, lo_ref,
                      m_ref, l_ref, acc_ref, *, tk, n_kt, topk):
    tq = QLANE_TQ
    q0 = pl.program_id(1) * tq
    nkt = _visible_key_tiles(q0, tq, tk, n_kt)
    q_chunk = (q0 + lax.broadcasted_iota(i32, (1, tq), 1)) >> CHUNK_SHIFT
    k_row = lax.broadcasted_iota(i32, (tk, 1), 0)

    iqs = _stack_heads_masked(iq_ref[...], IDX_DIM)
    iw_t = iw_ref[...].T
    w_h = [iw_t[h:h + 1, :] for h in range(N_HEADS)]

    def score_body(kt, _):
        ks = pl.multiple_of(kt * tk, tk)
        r = _dot_nt(ik_ref[pl.ds(ks, tk), :], iqs)
        score = jnp.zeros((tk, tq), f32)
        for h in range(N_HEADS):
            score = score + w_h[h] * jnp.maximum(r[:, h * tq:(h + 1) * tq], 0.0)
        score = jnp.where(score == 0.0, 0.0, score)
        bits = pltpu.bitcast(score, i32)
        key = jnp.where(bits < 0, bits ^ INT_MAX, bits)
        vis = ((ks + k_row) >> CHUNK_SHIFT) <= q_chunk
        key = jnp.where(vis, key, INT_MIN)
        key_ref[kt] = key
        hi_ref[kt] = (key >> 16).astype(i16)
        lo_ref[kt] = ((key & 0xFFFF) - 0x8000).astype(i16)
        return 0

    lax.fori_loop(0, nkt, score_body, 0)

    def count(pred):
        def body(kt, acc):
            hit = jnp.where(pred(key_ref[kt], kt * tk), 1.0, 0.0)
            return acc + jnp.sum(hit.reshape(tk // COUNT_ROWS, COUNT_ROWS, tq), axis=0)
        acc = lax.fori_loop(0, nkt, body, jnp.zeros((COUNT_ROWS, tq), f32))
        return jnp.sum(acc, axis=0, keepdims=True)

    def count16(ref, pred):
        def body(kt, acc):
            hit = jnp.where(pred(ref[kt]), jnp.int16(1), jnp.int16(0))
            for g in range(tk // COUNT_ROWS):
                acc = acc + hit[g * COUNT_ROWS:(g + 1) * COUNT_ROWS]
            return acc
        acc = lax.fori_loop(0, nkt, body, jnp.zeros((COUNT_ROWS, tq), i16))
        return jnp.sum(acc.astype(f32), axis=0, keepdims=True)

    def select16(ref, want):
        def body(b, t):
            cand = t + lax.shift_left(jnp.int32(1), 15 - b)
            cand16 = cand.astype(i16)
            return jnp.where(count16(ref, lambda v: v >= cand16) >= want, cand, t)
        return lax.fori_loop(0, 16, body, jnp.full((1, tq), -0x8000, i32))

    kf = float(topk)
    thr_hi = select16(hi_ref, kf)
    thr_hi16 = thr_hi.astype(i16)
    n_gt_hi = count16(hi_ref, lambda v: v > thr_hi16)

    def low_of_equals(kt, _):
        lo_ref[kt] = jnp.where(hi_ref[kt] == thr_hi16, lo_ref[kt], jnp.int16(-0x8000))
        return 0

    lax.fori_loop(0, nkt, low_of_equals, 0)
    thr_lo = select16(lo_ref, kf - n_gt_hi)
    thr = (thr_hi << 16) | (thr_lo + 0x8000)
    n_gt = count(lambda keys, j0: keys > thr)
    n_ge = count(lambda keys, j0: keys >= thr)
    need = kf - n_gt
    real = thr != INT_MIN

    def tie_index():
        def body(b, x):
            cand = x + lax.shift_left(jnp.int32(1), 14 - b)
            n_eq = count(lambda keys, j0: jnp.logical_and(keys == thr, (j0 + k_row) < cand))
            return jnp.where(n_eq < need, cand, x)
        return lax.fori_loop(0, 15, body, jnp.zeros((1, tq), i32))

    has_ties = jnp.max(jnp.where(jnp.logical_and(real, n_ge > kf), 1.0, 0.0)) > 0.0
    last_eq = lax.cond(has_ties, tie_index, lambda: jnp.full((1, tq), INT_MAX, i32))
    last_eq = jnp.where(real, last_eq, -1)

    qs = _stack_heads_masked(q_ref[...], D_HEAD)
    m_ref[...] = jnp.full_like(m_ref, NEG_BIG)
    l_ref[...] = jnp.zeros_like(l_ref)
    acc_ref[...] = jnp.zeros_like(acc_ref)

    def att_body(kt, _):
        ks = pl.multiple_of(kt * tk, tk)
        keys = key_ref[kt]
        sel = jnp.logical_or(keys > thr, jnp.logical_and(keys == thr, (ks + k_row) <= last_eq))
        bias = jnp.where(sel, 0.0, NEG_BIG)
        s = _dot_nt(k_ref[pl.ds(ks, tk), :], qs) + jnp.concatenate([bias] * N_HEADS, axis=1)
        m_old = m_ref[...]
        m_new = jnp.maximum(m_old, jnp.max(s, axis=0, keepdims=True))
        alpha = jnp.exp2(m_old - m_new)
        p = jnp.exp2(s - m_new)
        l_ref[...] = alpha * l_ref[...] + jnp.sum(p, axis=0, keepdims=True)
        acc_ref[...] = alpha * acc_ref[...] + _dot(vt_ref[kt], p.astype(bf16))
        m_ref[...] = m_new
        return 0

    lax.fori_loop(0, nkt, att_body, 0)
    acc = acc_ref[...]
    l = l_ref[...]
    out_t = jnp.concatenate(
        [acc[D_HEAD * h:D_HEAD * (h + 1), h * tq:(h + 1) * tq] / l[:, h * tq:(h + 1) * tq]
         for h in range(N_HEADS)], axis=0)
    o_ref[...] = out_t.T


def _resident(shape, index_map):
    return pl.BlockSpec(shape, index_map, pipeline_mode=pl.Buffered(1))


def _mixers(geom, bf_all, f_all, keys, w_uv):
    batch, tq, nq, tk, lk = geom["batch"], geom["tq"], geom["nq"], geom["tk"], geom["lk"]
    k_off, topk, q_blk0 = geom["k_off"], geom["topk"], geom["q_row0"] // geom["tq"]
    n_kt = lk // tk
    qlane = keys is None and tq == QLANE_TQ and k_off == 0 and batch == 1
    bf3 = bf_all.reshape(1, *bf_all.shape)
    f3 = f_all.reshape(1, *f_all.shape)

    def q_spec(width, col):
        return pl.BlockSpec((None, tq, width), lambda b, i: (0, q_blk0 + b * nq + i, col // width))

    def key_operand(name, packed_col, width):
        if keys is None:
            return bf3, _resident((None, lk, width), lambda b, i: (0, 0, packed_col // width))
        return keys[name], _resident((None, lk, width), lambda b, i: (b, 0, 0))

    out_shape = jax.ShapeDtypeStruct((batch, nq * tq, BRANCH_W), f32)
    out_spec = pl.BlockSpec((None, tq, BRANCH_W), lambda b, i: (b, i, 0))
    params = _cparams(("parallel", "arbitrary"))

    sbk, sbk_spec = key_operand("sb_k", B_SBK, 256)
    sbv, sbv_spec = key_operand("sb_v", B_SBV, 256)
    o_sb = pl.pallas_call(
        functools.partial(_sb_kernel, tq=tq, tk=128, k_off=k_off),
        out_shape=out_shape, grid=(batch, nq),
        in_specs=[q_spec(256, B_SBQ), sbk_spec, sbv_spec], out_specs=out_spec,
        scratch_shapes=[pltpu.VMEM((N_HEADS * tq, 256), f32), pltpu.VMEM((N_HEADS * tq, 1), f32)],
        compiler_params=params, name="sb_attention",
    )(bf3, sbk, sbv)

    kc, kc_spec = key_operand("kc", B_KC, 256)
    wuv_spec = pl.BlockSpec((N_HEADS, MLA_KV_LORA, BRANCH_W), lambda b, i: (0, 0, 0))
    if qlane:
        tiles_t = lambda col, width: jnp.transpose(
            bf_all[:lk, col:col + width].reshape(n_kt, tk, width), (0, 2, 1))
        tile_spec = lambda width: _resident((n_kt, width, tk), lambda b, i: (0, 0, 0))
        o_mla = pl.pallas_call(
            functools.partial(_mla_qlane_kernel, tk=tk, n_kt=n_kt),
            out_shape=out_shape, grid=(batch, nq),
            in_specs=[q_spec(1024, B_QA), kc_spec, tile_spec(MLA_KV_LORA), wuv_spec],
            out_specs=out_spec,
            scratch_shapes=[pltpu.VMEM((1, N_HEADS * tq), f32), pltpu.VMEM((1, N_HEADS * tq), f32),
                            pltpu.VMEM((MLA_KV_LORA, N_HEADS * tq), f32)],
            compiler_params=params, name="mla_attention_qlane",
        )(bf3, kc, tiles_t(B_KC, MLA_KV_LORA), w_uv)
    else:
        o_mla = pl.pallas_call(
            functools.partial(_mla_kernel, tq=tq, tk=tk, k_off=k_off, n_kt=n_kt),
            out_shape=out_shape, grid=(batch, nq),
            in_specs=[q_spec(1024, B_QA), kc_spec, wuv_spec],
            out_specs=out_spec,
            scratch_shapes=[pltpu.VMEM((N_HEADS * tq, 1), f32), pltpu.VMEM((N_HEADS * tq, 1), f32),
                            pltpu.VMEM((N_HEADS * tq, MLA_KV_LORA), f32)],
            compiler_params=params, name="mla_attention",
        )(bf3, kc, w_uv)

    dk, dk_spec = key_operand("dsa_k", B_DK, 256)
    dv, dv_spec = key_operand("dsa_v", B_DV, 256)
    ik, ik_spec = key_operand("ik", B_IK, 128)
    iw_spec = pl.BlockSpec((None, tq, 128), lambda b, i: (0, q_blk0 + b * nq + i, F_IW // 128))
    if qlane:
        o_dsa = pl.pallas_call(
            functools.partial(_dsa_qlane_kernel, tk=tk, n_kt=n_kt, topk=topk),
            out_shape=out_shape, grid=(batch, nq),
            in_specs=[q_spec(256, B_DQ), q_spec(128, B_IQ), iw_spec, dk_spec, tile_spec(BRANCH_W), ik_spec],
            out_specs=out_spec,
            scratch_shapes=[pltpu.VMEM((n_kt, tk, tq), i32),
                            pltpu.VMEM((n_kt, tk, tq), i16), pltpu.VMEM((n_kt, tk, tq), i16),
                            pltpu.VMEM((1, N_HEADS * tq), f32), pltpu.VMEM((1, N_HEADS * tq), f32),
                            pltpu.VMEM((BRANCH_W, N_HEADS * tq), f32)],
            compiler_params=params, name="dsa_attention_qlane",
        )(bf3, bf3, f3, dk, tiles_t(B_DV, BRANCH_W), ik)
    else:
        o_dsa = pl.pallas_call(
            functools.partial(_dsa_kernel, tq=tq, tk=tk, k_off=k_off, n_kt=n_kt, topk=topk),
            out_shape=out_shape, grid=(batch, nq),
            in_specs=[q_spec(256, B_DQ), q_spec(128, B_IQ), iw_spec, dk_spec, dv_spec, ik_spec],
            out_specs=out_spec,
            scratch_shapes=[pltpu.VMEM((n_kt, tq, tk), i32),
                            pltpu.VMEM((N_HEADS, tq, 1), f32), pltpu.VMEM((N_HEADS, tq, 1), f32),
                            pltpu.VMEM((N_HEADS, tq, BRANCH_W), f32)],
            compiler_params=params, name="dsa_attention",
        )(bf3, bf3, f3, dk, dv, ik)
    return o_sb, o_mla, o_dsa


def _rope_tables(pos):
    n = pos.shape[0]

    def pair(rot, per):
        half = rot // 2
        inv = jnp.float32(ROPE_THETA) ** (-2.0 * jnp.arange(half, dtype=f32) / rot)
        ang = pos.astype(f32)[:, None] * inv[None, :]
        cos, sin = jnp.cos(ang), jnp.sin(ang)
        c = jnp.concatenate([cos, cos, jnp.ones((n, per - rot), f32)], axis=-1)
        s = jnp.concatenate([-sin, sin, jnp.zeros((n, per - rot), f32)], axis=-1)
        return jnp.tile(c, (1, 128 // per)), jnp.tile(s, (1, 128 // per))

    return jnp.concatenate([*pair(DSA_ROT, D_HEAD), *pair(MLA_ROPE, MLA_ROPE), *pair(IDX_ROT, IDX_DIM)], axis=-1)


def _prep_layer_weights(w_in, w_uq, w_ukv):
    offs = [0]
    for s in IN_SPLITS:
        offs.append(offs[-1] + s)
    (sb_q, sb_k, sb_v, mla_dq, mla_dkv, mla_kr, idx_q, idx_w, idx_k,
     dsa_q, dsa_k, dsa_v, gate) = [w_in[:, offs[j]:offs[j + 1]] for j in range(len(IN_SPLITS))]
    small = jnp.concatenate([idx_w * 0.5, jnp.zeros((D_MODEL, 124), f32)], axis=-1)
    w_main = jnp.concatenate(
        [sb_q, sb_k, sb_v, dsa_q, dsa_k, dsa_v, mla_dq, mla_dkv, idx_q,
         jnp.tile(idx_k, (1, 4)), jnp.tile(mla_kr, (1, 4)), small], axis=-1).astype(bf16)
    uq = w_uq.reshape(MLA_Q_LORA, N_HEADS, MLA_NOPE + MLA_ROPE)
    w_uq_perm = jnp.concatenate([uq[:, :, :MLA_NOPE].reshape(MLA_Q_LORA, 256),
                                 uq[:, :, MLA_NOPE:].reshape(MLA_Q_LORA, 128)], axis=-1).astype(bf16)
    ukv = w_ukv.reshape(MLA_KV_LORA, N_HEADS, 128)
    w_ukt = jnp.transpose(ukv[:, :, :MLA_NOPE], (1, 2, 0)).reshape(256, MLA_KV_LORA).astype(bf16)
    w_uv = jnp.zeros((N_HEADS, MLA_KV_LORA, BRANCH_W), f32)
    for h in range(N_HEADS):
        w_uv = w_uv.at[h, :, D_HEAD * h:D_HEAD * (h + 1)].set(ukv[:, h, MLA_NOPE:])
    return w_main, gate.astype(bf16), w_uq_perm, w_ukt, w_uv.astype(bf16)


def _pick_tile(n, candidates):
    for c in candidates:
        if n % c == 0:
            return c
    raise ValueError(f"no tile in {candidates} divides {n}")


def kernel(x_prompt, x_sample, cache_sb_k, cache_sb_v, cache_mla_ckv, cache_mla_krope, cache_dsa_k, cache_dsa_v, cache_dsa_idx_k, g_ffn1_pre, g_ffn1_post, w_ffn1_gate_up, w_ffn1_down, g_mix_pre, g_mix_post, w_in, b_gate, g_mla_q, g_mla_kv, w_mla_uq, w_mla_ukv, w_branch, w_out, g_ffn2_pre, g_ffn2_post, w_ffn2_gate_up, w_ffn2_down):
    bp, tp, _ = x_prompt.shape
    bs, ts, _ = x_sample.shape
    depth = w_in.shape[0]
    past = cache_sb_k.shape[2]
    assert bp == 1 and tp % 128 == 0 and ts % 64 == 0 and ts <= 128
    n_p, n_s = bp * tp, bs * ts
    n = n_p + n_s
    tm = _pick_tile(n, (512, 256, 128, 64))
    ff_chunk = D_FF // 2

    pos = jnp.concatenate([jnp.arange(tp, dtype=i32), jnp.tile(past + jnp.arange(ts, dtype=i32), bs)])
    tab = _rope_tables(pos)

    tk_p = _pick_tile(tp, (1024, 512, 256, 128))
    lk_s = -(-(past + ts) // 128) * 128
    geom_p = dict(batch=1, tq=128, nq=tp // 128, tk=tk_p, lk=tp, k_off=0,
                  topk=min(TOPK_MAX, tp // 4), q_row0=0)
    geom_s = dict(batch=bs, tq=ts, nq=1, tk=128, lk=lk_s, k_off=past,
                  topk=min(TOPK_MAX, (past + ts) // 4), q_row0=n_p)

    y = jnp.concatenate([x_prompt.reshape(n_p, D_MODEL), x_sample.reshape(n_s, D_MODEL)], axis=0)
    row = lambda g: g.reshape(1, -1)
    rows_p, rows_s = [], []
    for l in range(depth):
        w_main, w_gate, w_uq_perm, w_ukt, w_uv = _prep_layer_weights(w_in[l], w_mla_uq[l], w_mla_ukv[l])
        h = _ffn(y, row(g_ffn1_pre[l]), row(g_ffn1_post[l]),
                 w_ffn1_gate_up[l][:, :D_FF].astype(bf16), w_ffn1_gate_up[l][:, D_FF:].astype(bf16),
                 w_ffn1_down[l].astype(bf16), tm, ff_chunk)
        f_all, bf_all = _inproj(h, tab, row(g_mix_pre[l]), w_main, row(g_mla_q[l]), row(g_mla_kv[l]),
                                w_uq_perm, w_ukt, tm)

        def with_cache(cache_rows, new_cols, width):
            new = bf_all[n_p:, new_cols:new_cols + width].reshape(bs, ts, width)
            allk = jnp.concatenate([cache_rows.astype(bf16), new], axis=1)
            return jnp.pad(allk, ((0, 0), (0, lk_s - past - ts), (0, 0)))

        keys_s = {
            "sb_k": with_cache(cache_sb_k[l].reshape(bs, past, 256), B_SBK, 256),
            "sb_v": with_cache(cache_sb_v[l].reshape(bs, past, 256), B_SBV, 256),
            "kc": with_cache(jnp.concatenate([cache_mla_ckv[l], jnp.tile(cache_mla_krope[l], (1, 1, 4))], axis=-1),
                             B_KC, 256),
            "dsa_k": with_cache(cache_dsa_k[l].reshape(bs, past, 256), B_DK, 256),
            "dsa_v": with_cache(cache_dsa_v[l].reshape(bs, past, 256), B_DV, 256),
            "ik": with_cache(jnp.tile(cache_dsa_idx_k[l], (1, 1, 4)), B_IK, 128),
        }
        o_p = _mixers(geom_p, bf_all, f_all, None, w_uv)
        o_s = _mixers(geom_s, bf_all, f_all, keys_s, w_uv)
        o_sb, o_mla, o_dsa = [jnp.concatenate([a.reshape(n_p, BRANCH_W), b.reshape(n_s, BRANCH_W)], axis=0)
                              for a, b in zip(o_p, o_s)]
        h2 = _merge(h, o_sb, o_mla, o_dsa, row(g_mix_pre[l]), row(g_mix_post[l]), w_gate, row(b_gate[l]),
                    w_branch[l].astype(bf16), w_out[l].astype(bf16), tm)
        y = _ffn(h2, row(g_ffn2_pre[l]), row(g_ffn2_post[l]),
                 w_ffn2_gate_up[l][:, :D_FF].astype(bf16), w_ffn2_gate_up[l][:, D_FF:].astype(bf16),
                 w_ffn2_down[l].astype(bf16), tm, ff_chunk)

        def new_rows(lo, hi, cnt, shape):
            return f_all[lo:hi, :].reshape(cnt + (W_F32,)), shape

        for rows, lo, hi, lead in ((rows_p, 0, n_p, (bp, tp)), (rows_s, n_p, n, (bs, ts))):
            blk = f_all[lo:hi]
            rows.append((
                blk[:, F_SBK:F_SBK + 256].reshape(*lead, N_HEADS, D_HEAD),
                blk[:, F_SBV:F_SBV + 256].reshape(*lead, N_HEADS, D_HEAD),
                blk[:, F_CKV:F_CKV + MLA_KV_LORA].reshape(*lead, MLA_KV_LORA),
                blk[:, F_KR:F_KR + MLA_ROPE].reshape(*lead, MLA_ROPE),
                blk[:, F_DK:F_DK + 256].reshape(*lead, N_HEADS, D_HEAD),
                blk[:, F_DV:F_DV + 256].reshape(*lead, N_HEADS, D_HEAD),
                blk[:, F_IK:F_IK + IDX_DIM].reshape(*lead, IDX_DIM),
            ))

    stack = lambda rows: [jnp.stack([r[j] for r in rows]) for j in range(7)]
    return (y[:n_p].reshape(bp, tp, D_MODEL), y[n_p:].reshape(bs, ts, D_MODEL), *stack(rows_p), *stack(rows_s))
```

```python
import functools

import jax
import jax.numpy as jnp
from jax import lax
from jax.experimental import pallas as pl
from jax.experimental.pallas import tpu as pltpu

D_MODEL = 1024
D_FF = 2816
CHUNK_SHIFT = 6
ROPE_THETA = 500000.0
RMS_EPS = 1e-6
D_HEAD = 64
N_HEADS = 4
BRANCH_W = 256
MLA_Q_LORA = 256
MLA_KV_LORA = 128
MLA_NOPE = 64
MLA_ROPE = 32
IDX_DIM = 32
DSA_ROT = 16
IDX_ROT = 8
TOPK_MAX = 256
N_BRANCH = 3
IN_SPLITS = (256, 256, 256, 256, 128, 32, 128, 4, 32, 256, 256, 256, 3072)

C_SBQ, C_SBK, C_SBV, C_DQ, C_DK, C_DV, C_MDQ = 0, 256, 512, 768, 1024, 1280, 1536
C_MDKV, C_IQ, C_IK, C_KR, C_IW, W_MAIN = 1792, 1920, 2048, 2176, 2304, 2432
F_SBK, F_SBV, F_DK, F_DV, F_CKV, F_KR, F_IK, F_IW, W_F32 = 0, 256, 512, 768, 1024, 1152, 1280, 1408, 1536
B_SBQ, B_SBK, B_SBV, B_DQ, B_DK, B_DV, B_KC, B_IQ, B_IK, B_QA, W_BF = (
    0, 256, 512, 768, 1024, 1280, 1536, 1792, 1920, 2048, 3072)

VMEM_LIMIT_V7X = 56 * 1024 * 1024
EXP_UNDERFLOW = -110.0
NEG_BIG = -1e30
INT_MIN = -2 ** 31
INT_MAX = 2 ** 31 - 1
LOG2E = 1.4426950408889634
MLA_Q_SCALE = (MLA_NOPE + MLA_ROPE) ** -0.5 * LOG2E
DSA_Q_SCALE = D_HEAD ** -0.5 * LOG2E
SB_Q_SCALE = D_HEAD ** -0.5
QLANE_TQ = 128
COUNT_ROWS = 64
RANK_ROWS = 128
QLANE_COL_BLOCKS = 1
MLA_ROW_BLOCKS = 2

f32 = jnp.float32
bf16 = jnp.bfloat16
i32 = jnp.int32


def _rms(x, g):
    return x * lax.rsqrt(jnp.mean(x * x, axis=-1, keepdims=True) + RMS_EPS) * g


def _dot(a, b):
    return jnp.dot(a, b, preferred_element_type=f32)


def _dot_nt(a, b):
    return lax.dot_general(a, b, (((1,), (1,)), ((), ())), preferred_element_type=f32)


def _lane_group_mask(width, group, h):
    lane = lax.broadcasted_iota(i32, (1, width), 1)
    return (lane // group) == h


def _rope(x, c, s, half, per):
    w = x.shape[-1]
    lane = lax.broadcasted_iota(i32, (1, w), 1)
    first = (lane & (per - 1)) < half
    partner = jnp.where(first, pltpu.roll(x, w - half, 1), pltpu.roll(x, half, 1))
    return x * c + partner * s


def _cparams(sem):
    return pltpu.CompilerParams(dimension_semantics=sem, vmem_limit_bytes=VMEM_LIMIT_V7X)


def _ffn_kernel(x_ref, gpre_ref, gpost_ref, wg_ref, wu_ref, wd_ref, o_ref, xn_ref, acc_ref):
    c = pl.program_id(1)

    @pl.when(c == 0)
    def _():
        xn_ref[...] = _rms(x_ref[...], gpre_ref[...]).astype(bf16)
        acc_ref[...] = jnp.zeros_like(acc_ref)

    xn = xn_ref[...]
    g = _dot(xn, wg_ref[...])
    u = _dot(xn, wu_ref[...])
    a = (g / (1.0 + jnp.exp(-g)) * u).astype(bf16)
    acc_ref[...] += _dot(a, wd_ref[...])

    @pl.when(c == pl.num_programs(1) - 1)
    def _():
        o_ref[...] = x_ref[...] + 0.5 * _rms(acc_ref[...], gpost_ref[...])


def _ffn(x, g_pre, g_post, wg, wu, wd, tm, ff_chunk):
    n = x.shape[0]
    nc = D_FF // ff_chunk
    return pl.pallas_call(
        _ffn_kernel,
        out_shape=jax.ShapeDtypeStruct((n, D_MODEL), f32),
        grid=(n // tm, nc),
        in_specs=[
            pl.BlockSpec((tm, D_MODEL), lambda i, c: (i, 0)),
            pl.BlockSpec((1, D_MODEL), lambda i, c: (0, 0)),
            pl.BlockSpec((1, D_MODEL), lambda i, c: (0, 0)),
            pl.BlockSpec((D_MODEL, ff_chunk), lambda i, c: (0, c)),
            pl.BlockSpec((D_MODEL, ff_chunk), lambda i, c: (0, c)),
            pl.BlockSpec((ff_chunk, D_MODEL), lambda i, c: (c, 0)),
        ],
        out_specs=pl.BlockSpec((tm, D_MODEL), lambda i, c: (i, 0)),
        scratch_shapes=[pltpu.VMEM((tm, D_MODEL), bf16), pltpu.VMEM((tm, D_MODEL), f32)],
        compiler_params=_cparams(("parallel", "arbitrary")),
        name="ffn_half_step",
    )(x, g_pre, g_post, wg, wu, wd)


def _inproj_kernel(h_ref, tab_ref, gmix_ref, wmain_ref, gq_ref, gkv_ref, wuq_ref, wukt_ref,
                   f_ref, b_ref):
    u = _rms(h_ref[...], gmix_ref[...]).astype(bf16)
    proj = _dot(u, wmain_ref[...])
    tab = tab_ref[...]
    c_dsa = jnp.concatenate([tab[:, 0:128]] * 2, axis=-1)
    s_dsa = jnp.concatenate([tab[:, 128:256]] * 2, axis=-1)
    c_mla, s_mla = tab[:, 256:384], tab[:, 384:512]
    c_idx, s_idx = tab[:, 512:640], tab[:, 640:768]

    sb_q = proj[:, C_SBQ:C_SBQ + 256]
    sb_k = proj[:, C_SBK:C_SBK + 256]
    sb_v = proj[:, C_SBV:C_SBV + 256]
    dsa_q = _rope(proj[:, C_DQ:C_DQ + 256], c_dsa, s_dsa, DSA_ROT // 2, D_HEAD)
    dsa_k = _rope(proj[:, C_DK:C_DK + 256], c_dsa, s_dsa, DSA_ROT // 2, D_HEAD)
    dsa_v = proj[:, C_DV:C_DV + 256]
    idx_q = _rope(proj[:, C_IQ:C_IQ + 128], c_idx, s_idx, IDX_ROT // 2, IDX_DIM)
    idx_k = _rope(proj[:, C_IK:C_IK + 128], c_idx, s_idx, IDX_ROT // 2, IDX_DIM)
    k_rope = _rope(proj[:, C_KR:C_KR + 128], c_mla, s_mla, MLA_ROPE // 2, MLA_ROPE)
    idx_w = proj[:, C_IW:C_IW + 128]
    c_kv = _rms(proj[:, C_MDKV:C_MDKV + 128], gkv_ref[...])

    q_n = _rms(proj[:, C_MDQ:C_MDQ + 256], gq_ref[...]).astype(bf16)
    q_m = _dot(q_n, wuq_ref[...])
    q_nope = q_m[:, :256]
    q_rope = _rope(q_m[:, 256:384], c_mla, s_mla, MLA_ROPE // 2, MLA_ROPE)
    wukt = wukt_ref[...]
    for h in range(N_HEADS):
        q_nh = jnp.where(_lane_group_mask(256, MLA_NOPE, h), q_nope, 0.0).astype(bf16)
        q_lat = _dot(q_nh, wukt)
        q_rh = jnp.where(_lane_group_mask(128, MLA_ROPE, h), q_rope, 0.0)
        b_ref[:, B_QA + 256 * h:B_QA + 256 * h + 128] = (q_lat * MLA_Q_SCALE).astype(bf16)
        b_ref[:, B_QA + 256 * h + 128:B_QA + 256 * (h + 1)] = (q_rh * MLA_Q_SCALE).astype(bf16)

    f_ref[:, F_SBK:F_SBK + 256] = sb_k
    f_ref[:, F_SBV:F_SBV + 256] = sb_v
    f_ref[:, F_DK:F_DK + 256] = dsa_k
    f_ref[:, F_DV:F_DV + 256] = dsa_v
    f_ref[:, F_CKV:F_CKV + 128] = c_kv
    f_ref[:, F_KR:F_KR + 128] = k_rope
    f_ref[:, F_IK:F_IK + 128] = idx_k
    f_ref[:, F_IW:F_IW + 128] = idx_w

    b_ref[:, B_SBQ:B_SBQ + 256] = (sb_q * SB_Q_SCALE).astype(bf16)
    b_ref[:, B_SBK:B_SBK + 256] = sb_k.astype(bf16)
    b_ref[:, B_SBV:B_SBV + 256] = sb_v.astype(bf16)
    b_ref[:, B_DQ:B_DQ + 256] = (dsa_q * DSA_Q_SCALE).astype(bf16)
    b_ref[:, B_DK:B_DK + 256] = dsa_k.astype(bf16)
    b_ref[:, B_DV:B_DV + 256] = dsa_v.astype(bf16)
    b_ref[:, B_KC:B_KC + 128] = c_kv.astype(bf16)
    b_ref[:, B_KC + 128:B_KC + 256] = k_rope.astype(bf16)
    b_ref[:, B_IQ:B_IQ + 128] = idx_q.astype(bf16)
    b_ref[:, B_IK:B_IK + 128] = idx_k.astype(bf16)


def _inproj(h, tab, g_mix, w_main, g_q, g_kv, w_uq, w_ukt, tm):
    n = h.shape[0]
    const = lambda i: (0, 0)
    return pl.pallas_call(
        _inproj_kernel,
        out_shape=(jax.ShapeDtypeStruct((n, W_F32), f32), jax.ShapeDtypeStruct((n, W_BF), bf16)),
        grid=(n // tm,),
        in_specs=[
            pl.BlockSpec((tm, D_MODEL), lambda i: (i, 0)),
            pl.BlockSpec((tm, 768), lambda i: (i, 0)),
            pl.BlockSpec((1, D_MODEL), const),
            pl.BlockSpec((D_MODEL, W_MAIN), const),
            pl.BlockSpec((1, MLA_Q_LORA), const),
            pl.BlockSpec((1, MLA_KV_LORA), const),
            pl.BlockSpec((MLA_Q_LORA, 384), const),
            pl.BlockSpec((256, MLA_KV_LORA), const),
        ],
        out_specs=(pl.BlockSpec((tm, W_F32), lambda i: (i, 0)),
                   pl.BlockSpec((tm, W_BF), lambda i: (i, 0))),
        compiler_params=_cparams(("parallel",)),
        name="mixer_in_projection",
    )(h, tab, g_mix, w_main, g_q, g_kv, w_uq, w_ukt)


def _merge_kernel(h_ref, osb_ref, omla_ref, odsa_ref, gpre_ref, gpost_ref, wgate_ref, bgate_ref,
                  wbr_ref, wout_ref, o_ref):
    h = h_ref[...]
    u = _rms(h, gpre_ref[...]).astype(bf16)
    merged = jnp.zeros_like(h)
    for b, o_b in enumerate((osb_ref, omla_ref, odsa_ref)):
        pre = _dot(u, wgate_ref[:, b * D_MODEL:(b + 1) * D_MODEL]) + bgate_ref[:, b * D_MODEL:(b + 1) * D_MODEL]
        gate = 1.0 / (1.0 + jnp.exp(-pre))
        merged = merged + _dot(o_b[...].astype(bf16), wbr_ref[b]) * gate
    o_ref[...] = h + _rms(_dot(merged.astype(bf16), wout_ref[...]), gpost_ref[...])


def _merge(h, o_sb, o_mla, o_dsa, g_pre, g_post, w_gate, b_gate, w_branch, w_out, tm):
    n = h.shape[0]
    const = lambda i: (0, 0)
    row = lambda w: pl.BlockSpec((tm, w), lambda i: (i, 0))
    return pl.pallas_call(
        _merge_kernel,
        out_shape=jax.ShapeDtypeStruct((n, D_MODEL), f32),
        grid=(n // tm,),
        in_specs=[
            row(D_MODEL), row(BRANCH_W), row(BRANCH_W), row(BRANCH_W),
            pl.BlockSpec((1, D_MODEL), const),
            pl.BlockSpec((1, D_MODEL), const),
            pl.BlockSpec((D_MODEL, N_BRANCH * D_MODEL), const),
            pl.BlockSpec((1, N_BRANCH * D_MODEL), const),
            pl.BlockSpec((N_BRANCH, BRANCH_W, D_MODEL), lambda i: (0, 0, 0)),
            pl.BlockSpec((D_MODEL, D_MODEL), const),
        ],
        out_specs=row(D_MODEL),
        compiler_params=_cparams(("parallel",)),
        name="gated_branch_merge",
    )(h, o_sb, o_mla, o_dsa, g_pre, g_post, w_gate, b_gate, w_branch, w_out)


def _visible_key_tiles(q0, tq, tk, n_kt):
    k_end = (((q0 + tq - 1) >> CHUNK_SHIFT) + 1) << CHUNK_SHIFT
    return jnp.minimum((k_end + tk - 1) // tk, n_kt)


def _stack_heads_masked(q, group):
    qf = q.astype(f32)
    w = q.shape[-1]
    return jnp.concatenate(
        [jnp.where(_lane_group_mask(w, group, h), qf, 0.0).astype(bf16) for h in range(N_HEADS)], axis=0)


def _take_head_lanes(stacked, tq):
    out = jnp.zeros((tq, stacked.shape[-1]), f32)
    for h in range(N_HEADS):
        out = jnp.where(_lane_group_mask(stacked.shape[-1], D_HEAD, h), stacked[h * tq:(h + 1) * tq], out)
    return out


def _sb_kernel(q_ref, k_ref, v_ref, o_ref, acc_ref, car_ref, *, tq, tk, k_off):
    q0 = k_off + pl.program_id(1) * tq
    qs = _stack_heads_masked(q_ref[...], D_HEAD)
    row = lax.broadcasted_iota(i32, (N_HEADS * tq, 1), 0)
    q_pos = q0 + (row & (tq - 1))
    col = lax.broadcasted_iota(i32, (1, tk), 1)
    later = (lax.broadcasted_iota(i32, (tk, tk), 0) > lax.broadcasted_iota(i32, (tk, tk), 1)).astype(bf16)
    acc_ref[...] = jnp.zeros_like(acc_ref)
    car_ref[...] = jnp.zeros_like(car_ref)

    def body(carry):
        kt, _ = carry
        ks = pl.multiple_of(kt * tk, tk)
        k = k_ref[pl.ds(ks, tk), :]
        v = v_ref[pl.ds(ks, tk), :]
        z = _dot_nt(qs, k)
        vis = (ks + col) < q_pos
        sp = jnp.maximum(z, 0.0) + jnp.log1p(jnp.exp(-jnp.abs(z)))
        log_rest = jnp.where(vis, -sp, 0.0)
        hi = log_rest.astype(bf16)
        r1 = log_rest - hi.astype(f32)
        mid = r1.astype(bf16)
        lo = (r1 - mid.astype(f32)).astype(bf16)
        after = _dot(hi, later) + _dot(mid, later) + _dot(lo, later) + car_ref[...]
        w = jnp.where(vis, jnp.exp(z - sp + after), 0.0)
        acc_ref[...] += _dot(w.astype(bf16), v)
        car = car_ref[...] + jnp.sum(log_rest, axis=-1, keepdims=True)
        car_ref[...] = car
        done = (jnp.max(car) < EXP_UNDERFLOW).astype(i32)
        return kt - 1, done

    kt_last = (q0 + tq - 2) // tk
    lax.while_loop(lambda c: jnp.logical_and(c[0] >= 0, c[1] == 0), body, (kt_last, jnp.int32(0)))
    o_ref[...] = _take_head_lanes(acc_ref[...], tq)


def _mla_kernel(q_ref, kc_ref, wuv_ref, o_ref, m_ref, l_ref, acc_ref, *, tq, tk, k_off, n_kt):
    q0 = k_off + pl.program_id(1) * tq
    q = q_ref[...]
    qs = jnp.concatenate([q[:, 256 * h:256 * (h + 1)] for h in range(N_HEADS)], axis=0)
    row = lax.broadcasted_iota(i32, (N_HEADS * tq, 1), 0)
    q_chunk = (q0 + (row & (tq - 1))) >> CHUNK_SHIFT
    col = lax.broadcasted_iota(i32, (1, tk), 1)
    m_ref[...] = jnp.full_like(m_ref, NEG_BIG)
    l_ref[...] = jnp.zeros_like(l_ref)
    acc_ref[...] = jnp.zeros_like(acc_ref)
    rb = N_HEADS * tq // MLA_ROW_BLOCKS

    def step(kt, masked):
        ks = pl.multiple_of(kt * tk, tk)
        kc = kc_ref[pl.ds(ks, tk), :]
        for r in range(MLA_ROW_BLOCKS):
            rows = slice(r * rb, (r + 1) * rb)
            s = _dot_nt(qs[rows], kc)
            if masked:
                s = jnp.where(((ks + col) >> CHUNK_SHIFT) <= q_chunk[rows], s, NEG_BIG)
            m_old = m_ref[rows]
            m_new = jnp.maximum(m_old, jnp.max(s, axis=-1, keepdims=True))
            alpha = jnp.exp2(m_old - m_new)
            p = jnp.exp2(s - m_new)
            l_ref[rows] = alpha * l_ref[rows] + jnp.sum(p, axis=-1, keepdims=True)
            acc_ref[rows] = alpha * acc_ref[rows] + _dot(p.astype(bf16), kc[:, :MLA_KV_LORA])
            m_ref[rows] = m_new
        return 0

    n_full = (((q0 >> CHUNK_SHIFT) + 1) << CHUNK_SHIFT) // tk
    lax.fori_loop(0, n_full, lambda kt, c: step(kt, False), 0)
    lax.fori_loop(n_full, _visible_key_tiles(q0, tq, tk, n_kt), lambda kt, c: step(kt, True), 0)
    o_lat = (acc_ref[...] / l_ref[...]).astype(bf16)
    out = jnp.zeros((tq, BRANCH_W), f32)
    for h in range(N_HEADS):
        out = out + _dot(o_lat[h * tq:(h + 1) * tq], wuv_ref[h])
    o_ref[...] = out


def _mla_qlane_kernel(q_ref, kc_ref, kct_ref, wuv_ref, o_ref, m_ref, l_ref, acc_ref, *, tk, n_kt):
    tq = QLANE_TQ
    q0 = pl.program_id(1) * tq
    q = q_ref[...]
    qs = jnp.concatenate([q[:, 256 * h:256 * (h + 1)] for h in range(N_HEADS)], axis=0)
    lane = lax.broadcasted_iota(i32, (1, N_HEADS * tq), 1)
    q_chunk = (q0 + (lane & (tq - 1))) >> CHUNK_SHIFT
    k_row = lax.broadcasted_iota(i32, (tk, 1), 0)
    m_ref[...] = jnp.full_like(m_ref, NEG_BIG)
    l_ref[...] = jnp.zeros_like(l_ref)
    acc_ref[...] = jnp.zeros_like(acc_ref)

    cb = N_HEADS * tq // QLANE_COL_BLOCKS

    def step(kt, masked):
        ks = pl.multiple_of(kt * tk, tk)
        kc = kc_ref[pl.ds(ks, tk), :]
        kct = kct_ref[kt]
        for c in range(QLANE_COL_BLOCKS):
            cols = slice(c * cb, (c + 1) * cb)
            s = _dot_nt(kc, qs[cols])
            if masked:
                s = jnp.where(((ks + k_row) >> CHUNK_SHIFT) <= q_chunk[:, cols], s, NEG_BIG)
            m_old = m_ref[:, cols]
            m_new = jnp.maximum(m_old, jnp.max(s, axis=0, keepdims=True))
            alpha = jnp.exp2(m_old - m_new)
            p = jnp.exp2(s - m_new)
            l_ref[:, cols] = alpha * l_ref[:, cols] + jnp.sum(p, axis=0, keepdims=True)
            acc_ref[:, cols] = alpha * acc_ref[:, cols] + _dot(kct, p.astype(bf16))
            m_ref[:, cols] = m_new
        return 0

    n_full = (((q0 >> CHUNK_SHIFT) + 1) << CHUNK_SHIFT) // tk
    lax.fori_loop(0, n_full, lambda kt, c: step(kt, False), 0)
    lax.fori_loop(n_full, _visible_key_tiles(q0, tq, tk, n_kt), lambda kt, c: step(kt, True), 0)
    o_lat_t = acc_ref[...] / l_ref[...]
    out = jnp.zeros((tq, BRANCH_W), f32)
    for h in range(N_HEADS):
        out = out + _dot(o_lat_t[:, h * tq:(h + 1) * tq].T.astype(bf16), wuv_ref[h])
    o_ref[...] = out


def _dsa_kernel(q_ref, iq_ref, iw_ref, k_ref, v_ref, ik_ref, o_ref, key_ref, m_ref, l_ref, acc_ref,
                *, tq, tk, k_off, n_kt, topk):
    q0 = k_off + pl.program_id(1) * tq
    nkt = _visible_key_tiles(q0, tq, tk, n_kt)
    row = lax.broadcasted_iota(i32, (tq, 1), 0)
    q_chunk = (q0 + row) >> CHUNK_SHIFT
    col = lax.broadcasted_iota(i32, (1, tk), 1)
    lanes = tk // 128

    iqs = _stack_heads_masked(iq_ref[...], IDX_DIM)
    iw = iw_ref[...]
    w_h = [iw[:, h:h + 1] for h in range(N_HEADS)]

    def score_body(kt, _):
        ks = pl.multiple_of(kt * tk, tk)
        r = _dot_nt(iqs, ik_ref[pl.ds(ks, tk), :])
        score = jnp.zeros((tq, tk), f32)
        for h in range(N_HEADS):
            score = score + w_h[h] * jnp.maximum(r[h * tq:(h + 1) * tq], 0.0)
        score = jnp.where(score == 0.0, 0.0, score)
        bits = pltpu.bitcast(score, i32)
        key = jnp.where(bits < 0, bits ^ INT_MAX, bits)
        vis = ((ks + col) >> CHUNK_SHIFT) <= q_chunk
        key_ref[kt] = jnp.where(vis, key, INT_MIN)
        return 0

    lax.fori_loop(0, nkt, score_body, 0)

    def count(pred):
        def body(kt, acc):
            keys = key_ref[kt]
            for c in range(lanes):
                acc = acc + jnp.where(pred(keys[:, c * 128:(c + 1) * 128], kt * tk + c * 128), 1.0, 0.0)
            return acc
        acc = lax.fori_loop(0, nkt, body, jnp.zeros((tq, 128), f32))
        return jnp.sum(acc, axis=-1, keepdims=True)

    def count_ge(cand):
        cand_b = jnp.broadcast_to(cand, (tq, 128))
        return count(lambda keys, j0: keys >= cand_b)

    kf = float(topk)
    thr = jnp.where(count_ge(jnp.zeros((tq, 1), i32)) >= kf, 0, INT_MIN).astype(i32)

    def bit_body(b, thr):
        cand = thr + lax.shift_left(jnp.int32(1), 30 - b)
        return jnp.where(count_ge(cand) >= kf, cand, thr)

    thr = lax.fori_loop(0, 31, bit_body, thr)
    thr_b = jnp.broadcast_to(thr, (tq, 128))
    n_gt = count(lambda keys, j0: keys > thr_b)
    n_ge = count(lambda keys, j0: keys >= thr_b)
    need = kf - n_gt
    real = thr != INT_MIN
    lane = lax.broadcasted_iota(i32, (1, 128), 1)

    def tie_index():
        def body(b, x):
            cand = x + lax.shift_left(jnp.int32(1), 14 - b)
            cand_b = jnp.broadcast_to(cand, (tq, 128))
            n_eq = count(lambda keys, j0: jnp.logical_and(keys == thr_b, (j0 + lane) < cand_b))
            return jnp.where(n_eq < need, cand, x)
        return lax.fori_loop(0, 15, body, jnp.zeros((tq, 1), i32))

    has_ties = jnp.max(jnp.where(jnp.logical_and(real, n_ge > kf), 1.0, 0.0)) > 0.0
    last_eq = lax.cond(has_ties, tie_index, lambda: jnp.full((tq, 1), INT_MAX, i32))
    last_eq = jnp.where(real, last_eq, -1)

    q = q_ref[...].astype(f32)
    q_h = [jnp.where(_lane_group_mask(256, D_HEAD, h), q, 0.0).astype(bf16) for h in range(N_HEADS)]
    m_ref[...] = jnp.full_like(m_ref, NEG_BIG)
    l_ref[...] = jnp.zeros_like(l_ref)
    acc_ref[...] = jnp.zeros_like(acc_ref)

    def att_body(kt, _):
        ks = pl.multiple_of(kt * tk, tk)
        k = k_ref[pl.ds(ks, tk), :]
        v = v_ref[pl.ds(ks, tk), :]
        keys = key_ref[kt]
        sel = jnp.logical_or(keys > thr, jnp.logical_and(keys == thr, (ks + col) <= last_eq))
        for h in range(N_HEADS):
            s = jnp.where(sel, _dot_nt(q_h[h], k), NEG_BIG)
            m_old = m_ref[h]
            m_new = jnp.maximum(m_old, jnp.max(s, axis=-1, keepdims=True))
            alpha = jnp.exp2(m_old - m_new)
            p = jnp.where(sel, jnp.exp2(s - m_new), 0.0)
            l_ref[h] = alpha * l_ref[h] + jnp.sum(p, axis=-1, keepdims=True)
            acc_ref[h] = alpha * acc_ref[h] + _dot(p.astype(bf16), v)
            m_ref[h] = m_new
        return 0

    lax.fori_loop(0, nkt, att_body, 0)
    out = jnp.zeros((tq, BRANCH_W), f32)
    for h in range(N_HEADS):
        out = jnp.where(_lane_group_mask(BRANCH_W, D_HEAD, h), acc_ref[h] / l_ref[h], out)
    o_ref[...] = out


def _dsa_qlane_kernel(q_ref, iq_ref, iw_ref, k_ref, vt_ref, ik_ref, o_ref, key_ref, m_ref, l_ref, acc_ref,
                      *, tk, n_kt, topk):
    tq = QLANE_TQ
    q0 = pl.program_id(1) * tq
    nkt = _visible_key_tiles(q0, tq, tk, n_kt)
    q_chunk = (q0 + lax.broadcasted_iota(i32, (1, tq), 1)) >> CHUNK_SHIFT
    k_row = lax.broadcasted_iota(i32, (tk, 1), 0)

    iqs = _stack_heads_masked(iq_ref[...], IDX_DIM)
    iw_t = iw_ref[...].T
    w_h = [iw_t[h:h + 1, :] for h in range(N_HEADS)]

    def score_body(kt, _):
        ks = pl.multiple_of(kt * tk, tk)
        r = _dot_nt(ik_ref[pl.ds(ks, tk), :], iqs)
        score = jnp.zeros((tk, tq), f32)
        for h in range(N_HEADS):
            score = score + w_h[h] * jnp.maximum(r[:, h * tq:(h + 1) * tq], 0.0)
        score = jnp.where(score == 0.0, 0.0, score)
        bits = pltpu.bitcast(score, i32)
        key = jnp.where(bits < 0, bits ^ INT_MAX, bits)
        vis = ((ks + k_row) >> CHUNK_SHIFT) <= q_chunk
        key_ref[kt] = jnp.where(vis, key, INT_MIN)
        return 0

    lax.fori_loop(0, nkt, score_body, 0)

    def count(pred):
        def body(kt, acc):
            hit = jnp.where(pred(key_ref[kt], kt * tk), 1.0, 0.0)
            return acc + jnp.sum(hit.reshape(tk // COUNT_ROWS, COUNT_ROWS, tq), axis=0)
        acc = lax.fori_loop(0, nkt, body, jnp.zeros((COUNT_ROWS, tq), f32))
        return jnp.sum(acc, axis=0, keepdims=True)

    kf = float(topk)
    thr = jnp.where(count(lambda keys, j0: keys >= 0) >= kf, 0, INT_MIN).astype(i32)

    def bit_body(b, thr):
        cand = thr + lax.shift_left(jnp.int32(1), 30 - b)
        return jnp.where(count(lambda keys, j0: keys >= cand) >= kf, cand, thr)

    thr = lax.fori_loop(0, 31, bit_body, thr)
    n_gt = count(lambda keys, j0: keys > thr)
    need = jnp.where(thr != INT_MIN, kf - n_gt, 0.0)

    before = (lax.broadcasted_iota(i32, (RANK_ROWS, RANK_ROWS), 1)
              < lax.broadcasted_iota(i32, (RANK_ROWS, RANK_ROWS), 0)).astype(bf16)

    def rank_body(kt, seen):
        keys = key_ref[kt]
        equal = jnp.where(keys == thr, 1.0, 0.0)
        equal_b = equal.astype(bf16)
        bias = []
        for g in range(tk // RANK_ROWS):
            rows = slice(g * RANK_ROWS, (g + 1) * RANK_ROWS)
            rank = _dot(before, equal_b[rows]) + seen
            tie_bias = jnp.where(rank < need, 0.0, NEG_BIG)
            bias.append(jnp.where(keys[rows] > thr, 0.0, jnp.where(keys[rows] == thr, tie_bias, NEG_BIG)))
            seen = seen + jnp.sum(equal[rows], axis=0, keepdims=True)
        key_ref[kt] = pltpu.bitcast(jnp.concatenate(bias, axis=0), i32)
        return seen

    lax.fori_loop(0, nkt, rank_body, jnp.zeros((1, tq), f32))

    qs = _stack_heads_masked(q_ref[...], D_HEAD)
    m_ref[...] = jnp.full_like(m_ref, NEG_BIG)
    l_ref[...] = jnp.zeros_like(l_ref)
    acc_ref[...] = jnp.zeros_like(acc_ref)

    def att_body(kt, _):
        ks = pl.multiple_of(kt * tk, tk)
        bias = pltpu.bitcast(key_ref[kt], f32)
        s = _dot_nt(k_ref[pl.ds(ks, tk), :], qs) + jnp.concatenate([bias] * N_HEADS, axis=1)
        m_old = m_ref[...]
        m_new = jnp.maximum(m_old, jnp.max(s, axis=0, keepdims=True))
        alpha = jnp.exp2(m_old - m_new)
        p = jnp.exp2(s - m_new)
        l_ref[...] = alpha * l_ref[...] + jnp.sum(p, axis=0, keepdims=True)
        acc_ref[...] = alpha * acc_ref[...] + _dot(vt_ref[kt], p.astype(bf16))
        m_ref[...] = m_new
        return 0

    lax.fori_loop(0, nkt, att_body, 0)
    acc = acc_ref[...]
    l = l_ref[...]
    out_t = jnp.concatenate(
        [acc[D_HEAD * h:D_HEAD * (h + 1), h * tq:(h + 1) * tq] / l[:, h * tq:(h + 1) * tq]
         for h in range(N_HEADS)], axis=0)
    o_ref[...] = out_t.T


def _resident(shape, index_map):
    return pl.BlockSpec(shape, index_map, pipeline_mode=pl.Buffered(1))


def _mixers(geom, bf_all, f_all, keys, w_uv):
    batch, tq, nq, tk, lk = geom["batch"], geom["tq"], geom["nq"], geom["tk"], geom["lk"]
    k_off, topk, q_blk0 = geom["k_off"], geom["topk"], geom["q_row0"] // geom["tq"]
    n_kt = lk // tk
    qlane = keys is None and tq == QLANE_TQ and k_off == 0 and batch == 1
    bf3 = bf_all.reshape(1, *bf_all.shape)
    f3 = f_all.reshape(1, *f_all.shape)

    def q_spec(width, col):
        return pl.BlockSpec((None, tq, width), lambda b, i: (0, q_blk0 + b * nq + i, col // width))

    def key_operand(name, packed_col, width):
        if keys is None:
            return bf3, _resident((None, lk, width), lambda b, i: (0, 0, packed_col // width))
        return keys[name], _resident((None, lk, width), lambda b, i: (b, 0, 0))

    out_shape = jax.ShapeDtypeStruct((batch, nq * tq, BRANCH_W), f32)
    out_spec = pl.BlockSpec((None, tq, BRANCH_W), lambda b, i: (b, i, 0))
    params = _cparams(("parallel", "arbitrary"))

    sbk, sbk_spec = key_operand("sb_k", B_SBK, 256)
    sbv, sbv_spec = key_operand("sb_v", B_SBV, 256)
    o_sb = pl.pallas_call(
        functools.partial(_sb_kernel, tq=tq, tk=128, k_off=k_off),
        out_shape=out_shape, grid=(batch, nq),
        in_specs=[q_spec(256, B_SBQ), sbk_spec, sbv_spec], out_specs=out_spec,
        scratch_shapes=[pltpu.VMEM((N_HEADS * tq, 256), f32), pltpu.VMEM((N_HEADS * tq, 1), f32)],
        compiler_params=params, name="sb_attention",
    )(bf3, sbk, sbv)

    kc, kc_spec = key_operand("kc", B_KC, 256)
    wuv_spec = pl.BlockSpec((N_HEADS, MLA_KV_LORA, BRANCH_W), lambda b, i: (0, 0, 0))
    if qlane:
        tiles_t = lambda col, width: jnp.transpose(
            bf_all[:lk, col:col + width].reshape(n_kt, tk, width), (0, 2, 1))
        tile_spec = lambda width: _resident((n_kt, width, tk), lambda b, i: (0, 0, 0))
        o_mla = pl.pallas_call(
            functools.partial(_mla_qlane_kernel, tk=tk, n_kt=n_kt),
            out_shape=out_shape, grid=(batch, nq),
            in_specs=[q_spec(1024, B_QA), kc_spec, tile_spec(MLA_KV_LORA), wuv_spec],
            out_specs=out_spec,
            scratch_shapes=[pltpu.VMEM((1, N_HEADS * tq), f32), pltpu.VMEM((1, N_HEADS * tq), f32),
                            pltpu.VMEM((MLA_KV_LORA, N_HEADS * tq), f32)],
            compiler_params=params, name="mla_attention_qlane",
        )(bf3, kc, tiles_t(B_KC, MLA_KV_LORA), w_uv)
    else:
        o_mla = pl.pallas_call(
            functools.partial(_mla_kernel, tq=tq, tk=tk, k_off=k_off, n_kt=n_kt),
            out_shape=out_shape, grid=(batch, nq),
            in_specs=[q_spec(1024, B_QA), kc_spec, wuv_spec],
            out_specs=out_spec,
            scratch_shapes=[pltpu.VMEM((N_HEADS * tq, 1), f32), pltpu.VMEM((N_HEADS * tq, 1), f32),
                            pltpu.VMEM((N_HEADS * tq, MLA_KV_LORA), f32)],
            compiler_params=params, name="mla_attention",
        )(bf3, kc, w_uv)

    dk, dk_spec = key_operand("dsa_k", B_DK, 256)
    dv, dv_spec = key_operand("dsa_v", B_DV, 256)
    ik, ik_spec = key_operand("ik", B_IK, 128)
    iw_spec = pl.BlockSpec((None, tq, 128), lambda b, i: (0, q_blk0 + b * nq + i, F_IW // 128))
    if qlane:
        o_dsa = pl.pallas_call(
            functools.partial(_dsa_qlane_kernel, tk=tk, n_kt=n_kt, topk=topk),
            out_shape=out_shape, grid=(batch, nq),
            in_specs=[q_spec(256, B_DQ), q_spec(128, B_IQ), iw_spec, dk_spec, tile_spec(BRANCH_W), ik_spec],
            out_specs=out_spec,
            scratch_shapes=[pltpu.VMEM((n_kt, tk, tq), i32),
                            pltpu.VMEM((1, N_HEADS * tq), f32), pltpu.VMEM((1, N_HEADS * tq), f32),
                            pltpu.VMEM((BRANCH_W, N_HEADS * tq), f32)],
            compiler_params=params, name="dsa_attention_qlane",
        )(bf3, bf3, f3, dk, tiles_t(B_DV, BRANCH_W), ik)
    else:
        o_dsa = pl.pallas_call(
            functools.partial(_dsa_kernel, tq=tq, tk=tk, k_off=k_off, n_kt=n_kt, topk=topk),
            out_shape=out_shape, grid=(batch, nq),
            in_specs=[q_spec(256, B_DQ), q_spec(128, B_IQ), iw_spec, dk_spec, dv_spec, ik_spec],
            out_specs=out_spec,
            scratch_shapes=[pltpu.VMEM((n_kt, tq, tk), i32),
                            pltpu.VMEM((N_HEADS, tq, 1), f32), pltpu.VMEM((N_HEADS, tq, 1), f32),
                            pltpu.VMEM((N_HEADS, tq, BRANCH_W), f32)],
            compiler_params=params, name="dsa_attention",
        )(bf3, bf3, f3, dk, dv, ik)
    return o_sb, o_mla, o_dsa


def _rope_tables(pos):
    n = pos.shape[0]

    def pair(rot, per):
        half = rot // 2
        inv = jnp.float32(ROPE_THETA) ** (-2.0 * jnp.arange(half, dtype=f32) / rot)
        ang = pos.astype(f32)[:, None] * inv[None, :]
        cos, sin = jnp.cos(ang), jnp.sin(ang)
        c = jnp.concatenate([cos, cos, jnp.ones((n, per - rot), f32)], axis=-1)
        s = jnp.concatenate([-sin, sin, jnp.zeros((n, per - rot), f32)], axis=-1)
        return jnp.tile(c, (1, 128 // per)), jnp.tile(s, (1, 128 // per))

    return jnp.concatenate([*pair(DSA_ROT, D_HEAD), *pair(MLA_ROPE, MLA_ROPE), *pair(IDX_ROT, IDX_DIM)], axis=-1)


def _prep_layer_weights(w_in, w_uq, w_ukv):
    offs = [0]
    for s in IN_SPLITS:
        offs.append(offs[-1] + s)
    (sb_q, sb_k, sb_v, mla_dq, mla_dkv, mla_kr, idx_q, idx_w, idx_k,
     dsa_q, dsa_k, dsa_v, gate) = [w_in[:, offs[j]:offs[j + 1]] for j in range(len(IN_SPLITS))]
    small = jnp.concatenate([idx_w * 0.5, jnp.zeros((D_MODEL, 124), f32)], axis=-1)
    w_main = jnp.concatenate(
        [sb_q, sb_k, sb_v, dsa_q, dsa_k, dsa_v, mla_dq, mla_dkv, idx_q,
         jnp.tile(idx_k, (1, 4)), jnp.tile(mla_kr, (1, 4)), small], axis=-1).astype(bf16)
    uq = w_uq.reshape(MLA_Q_LORA, N_HEADS, MLA_NOPE + MLA_ROPE)
    w_uq_perm = jnp.concatenate([uq[:, :, :MLA_NOPE].reshape(MLA_Q_LORA, 256),
                                 uq[:, :, MLA_NOPE:].reshape(MLA_Q_LORA, 128)], axis=-1).astype(bf16)
    ukv = w_ukv.reshape(MLA_KV_LORA, N_HEADS, 128)
    w_ukt = jnp.transpose(ukv[:, :, :MLA_NOPE], (1, 2, 0)).reshape(256, MLA_KV_LORA).astype(bf16)
    w_uv = jnp.zeros((N_HEADS, MLA_KV_LORA, BRANCH_W), f32)
    for h in range(N_HEADS):
        w_uv = w_uv.at[h, :, D_HEAD * h:D_HEAD * (h + 1)].set(ukv[:, h, MLA_NOPE:])
    return w_main, gate.astype(bf16), w_uq_perm, w_ukt, w_uv.astype(bf16)


def _pick_tile(n, candidates):
    for c in candidates:
        if n % c == 0:
            return c
    raise ValueError(f"no tile in {candidates} divides {n}")


def kernel(x_prompt, x_sample, cache_sb_k, cache_sb_v, cache_mla_ckv, cache_mla_krope, cache_dsa_k, cache_dsa_v, cache_dsa_idx_k, g_ffn1_pre, g_ffn1_post, w_ffn1_gate_up, w_ffn1_down, g_mix_pre, g_mix_post, w_in, b_gate, g_mla_q, g_mla_kv, w_mla_uq, w_mla_ukv, w_branch, w_out, g_ffn2_pre, g_ffn2_post, w_ffn2_gate_up, w_ffn2_down):
    bp, tp, _ = x_prompt.shape
    bs, ts, _ = x_sample.shape
    depth = w_in.shape[0]
    past = cache_sb_k.shape[2]
    assert bp == 1 and tp % 128 == 0 and ts % 64 == 0 and ts <= 128
    n_p, n_s = bp * tp, bs * ts
    n = n_p + n_s
    tm = _pick_tile(n, (512, 256, 128, 64))
    ff_chunk = D_FF // 2

    pos = jnp.concatenate([jnp.arange(tp, dtype=i32), jnp.tile(past + jnp.arange(ts, dtype=i32), bs)])
    tab = _rope_tables(pos)

    tk_p = _pick_tile(tp, (1024, 512, 256, 128))
    lk_s = -(-(past + ts) // 128) * 128
    geom_p = dict(batch=1, tq=128, nq=tp // 128, tk=tk_p, lk=tp, k_off=0,
                  topk=min(TOPK_MAX, tp // 4), q_row0=0)
    geom_s = dict(batch=bs, tq=ts, nq=1, tk=128, lk=lk_s, k_off=past,
                  topk=min(TOPK_MAX, (past + ts) // 4), q_row0=n_p)

    y = jnp.concatenate([x_prompt.reshape(n_p, D_MODEL), x_sample.reshape(n_s, D_MODEL)], axis=0)
    row = lambda g: g.reshape(1, -1)
    rows_p, rows_s = [], []
    for l in range(depth):
        w_main, w_gate, w_uq_perm, w_ukt, w_uv = _prep_layer_weights(w_in[l], w_mla_uq[l], w_mla_ukv[l])
        h = _ffn(y, row(g_ffn1_pre[l]), row(g_ffn1_post[l]),
                 w_ffn1_gate_up[l][:, :D_FF].astype(bf16), w_ffn1_gate_up[l][:, D_FF:].astype(bf16),
                 w_ffn1_down[l].astype(bf16), tm, ff_chunk)
        f_all, bf_all = _inproj(h, tab, row(g_mix_pre[l]), w_main, row(g_mla_q[l]), row(g_mla_kv[l]),
                                w_uq_perm, w_ukt, tm)

        def with_cache(cache_rows, new_cols, width):
            new = bf_all[n_p:, new_cols:new_cols + width].reshape(bs, ts, width)
            allk = jnp.concatenate([cache_rows.astype(bf16), new], axis=1)
            return jnp.pad(allk, ((0, 0), (0, lk_s - past - ts), (0, 0)))

        keys_s = {
            "sb_k": with_cache(cache_sb_k[l].reshape(bs, past, 256), B_SBK, 256),
            "sb_v": with_cache(cache_sb_v[l].reshape(bs, past, 256), B_SBV, 256),
            "kc": with_cache(jnp.concatenate([cache_mla_ckv[l], jnp.tile(cache_mla_krope[l], (1, 1, 4))], axis=-1),
                             B_KC, 256),
            "dsa_k": with_cache(cache_dsa_k[l].reshape(bs, past, 256), B_DK, 256),
            "dsa_v": with_cache(cache_dsa_v[l].reshape(bs, past, 256), B_DV, 256),
            "ik": with_cache(jnp.tile(cache_dsa_idx_k[l], (1, 1, 4)), B_IK, 128),
        }
        o_p = _mixers(geom_p, bf_all, f_all, None, w_uv)
        o_s = _mixers(geom_s, bf_all, f_all, keys_s, w_uv)
        o_sb, o_mla, o_dsa = [jnp.concatenate([a.reshape(n_p, BRANCH_W), b.reshape(n_s, BRANCH_W)], axis=0)
                              for a, b in zip(o_p, o_s)]
        h2 = _merge(h, o_sb, o_mla, o_dsa, row(g_mix_pre[l]), row(g_mix_post[l]), w_gate, row(b_gate[l]),
                    w_branch[l].astype(bf16), w_out[l].astype(bf16), tm)
        y = _ffn(h2, row(g_ffn2_pre[l]), row(g_ffn2_post[l]),
                 w_ffn2_gate_up[l][:, :D_FF].astype(bf16), w_ffn2_gate_up[l][:, D_FF:].astype(bf16),
                 w_ffn2_down[l].astype(bf16), tm, ff_chunk)

        def new_rows(lo, hi, cnt, shape):
            return f_all[lo:hi, :].reshape(cnt + (W_F32,)), shape

        for rows, lo, hi, lead in ((rows_p, 0, n_p, (bp, tp)), (rows_s, n_p, n, (bs, ts))):
            blk = f_all[lo:hi]
            rows.append((
                blk[:, F_SBK:F_SBK + 256].reshape(*lead, N_HEADS, D_HEAD),
                blk[:, F_SBV:F_SBV + 256].reshape(*lead, N_HEADS, D_HEAD),
                blk[:, F_CKV:F_CKV + MLA_KV_LORA].reshape(*lead, MLA_KV_LORA),
                blk[:, F_KR:F_KR + MLA_ROPE].reshape(*lead, MLA_ROPE),
                blk[:, F_DK:F_DK + 256].reshape(*lead, N_HEADS, D_HEAD),
                blk[:, F_DV:F_DV + 256].reshape(*lead, N_HEADS, D_HEAD),
                blk[:, F_IK:F_IK + IDX_DIM].reshape(*lead, IDX_DIM),
            ))

    stack = lambda rows: [jnp.stack([r[j] for r in rows]) for j in range(7)]
    return (y[:n_p].reshape(bp, tp, D_MODEL), y[n_p:].reshape(bs, ts, D_MODEL), *stack(rows_p), *stack(rows_s))
```
